```python
import math
import jax, jax.numpy as jnp
from jax import lax
import numpy as np

D_MODEL = 1024
BATCH = 32
SEQ = 2048
DEPTH = 1

MIX_WIDTH = D_MODEL
SSM_WIDTH = MIX_WIDTH // 2
ATTN_WIDTH = MIX_WIDTH - SSM_WIDTH
SSM_GROUP = 16
SSM_GROUPS = SSM_WIDTH // SSM_GROUP
SSM_STATE = 64
HEAD_DIM = 64
N_HEADS = ATTN_WIDTH // HEAD_DIM
IDX_HEADS = 8
IDX_DIM = 64
TOPK_MAX = 256
QBLOCK = 32
NUM_BUCKETS = 32
MAX_DISTANCE = 128
EPS = 1e-6
DT_MIN = 1e-3
DT_MAX = 1e-1
SPLITS = (SSM_WIDTH, SSM_WIDTH, ATTN_WIDTH, ATTN_WIDTH, ATTN_WIDTH, ATTN_WIDTH,
          IDX_HEADS * IDX_DIM, IDX_DIM, IDX_HEADS)
IN_WIDTH = 2 * SSM_WIDTH + 4 * ATTN_WIDTH + IDX_HEADS * IDX_DIM + IDX_DIM + IDX_HEADS

kernel_name = "hybrid_s5_dsa_parallel_heads"


def rms_norm(x, g):
    xf = x.astype(jnp.float32)
    xf = xf * lax.rsqrt(jnp.mean(xf * xf, axis=-1, keepdims=True) + EPS)
    return (xf * g.astype(jnp.float32)).astype(x.dtype)


def t5_causal_bucket(dist):
    max_exact = NUM_BUCKETS // 2
    is_small = dist < max_exact
    d = jnp.maximum(dist, 1).astype(jnp.float32)
    large = max_exact + (jnp.log(d / max_exact) / math.log(MAX_DISTANCE / max_exact)
                         * (NUM_BUCKETS - max_exact)).astype(jnp.int32)
    large = jnp.minimum(large, NUM_BUCKETS - 1)
    return jnp.where(is_small, dist, large)


def _ssm_combine(e1, e2):
    a1, b1 = e1
    a2, b2 = e2
    return a1 * a2, a2 * b1 + b2


def s5_mixer(u, a_re, a_im, log_dt, b_re, b_im, c_re, c_im, d_skip, w_glu, b_glu):
    bsz, seq, _ = u.shape
    uf = u.astype(jnp.float32)
    ug = uf.reshape(bsz, seq, SSM_GROUPS, SSM_GROUP)
    lam = lax.complex(a_re.astype(jnp.float32), a_im.astype(jnp.float32))
    dt = jnp.exp(log_dt.astype(jnp.float32))[:, None]
    lam_bar = jnp.exp(lam * dt)
    b_mat = lax.complex(b_re.astype(jnp.float32), b_im.astype(jnp.float32))
    b_bar = ((lam_bar - 1.0) / lam)[..., None] * b_mat
    bu = jnp.einsum('gpc,bsgc->bsgp', b_bar, ug.astype(jnp.complex64))
    a_seq = jnp.broadcast_to(lam_bar, (1, seq) + lam_bar.shape)
    _, states = lax.associative_scan(_ssm_combine, (a_seq, bu), axis=1)
    c_mat = lax.complex(c_re.astype(jnp.float32), c_im.astype(jnp.float32))
    y = jnp.einsum('gcp,bsgp->bsgc', c_mat, states).real.reshape(bsz, seq, SSM_WIDTH)
    y = y + d_skip.astype(jnp.float32) * uf
    z = jax.nn.gelu(y)
    z = z * jax.nn.sigmoid(z @ w_glu.astype(jnp.float32) + b_glu.astype(jnp.float32))
    return z.astype(u.dtype)


def dsa_mixer(q, k, v, q_idx, k_idx, w_idx, q_gain, k_gain, rel_bias):
    bsz, seq = q.shape[0], q.shape[1]
    topk = min(TOPK_MAX, seq // 4)
    nb = seq // QBLOCK
    scale = HEAD_DIM ** -0.5
    q = rms_norm(q.reshape(bsz, seq, N_HEADS, HEAD_DIM), q_gain)
    k = rms_norm(k.reshape(bsz, seq, N_HEADS, HEAD_DIM), k_gain)
    v = v.reshape(bsz, seq, N_HEADS, HEAD_DIM)
    q_idx = q_idx.reshape(bsz, seq, IDX_HEADS, IDX_DIM)
    w_idx = w_idx * (IDX_HEADS ** -0.5)
    key_pos = jnp.arange(seq, dtype=jnp.int32)

    def to_blocks(t):
        return jnp.moveaxis(t.reshape((bsz, nb, QBLOCK) + t.shape[2:]), 1, 0)

    gather = jax.vmap(lambda src, idx: src[idx])

    def block(args):
        qb, qib, wib, pos = args
        rel = jnp.einsum('bqhd,bsd->bqhs', qib, k_idx)
        score = jnp.einsum('bqh,bqhs->bqs', wib, jax.nn.relu(rel)).astype(jnp.float32)
        causal = key_pos[None, :] <= pos[:, None]
        score = jnp.where(causal[None], score, -jnp.inf)
        _, sel = lax.top_k(score, topk)
        k_sel = gather(k, sel)
        v_sel = gather(v, sel)
        logits = jnp.einsum('bqhd,bqkhd->bqhk', qb, k_sel).astype(jnp.float32) * scale
        dist = pos[None, :, None] - sel
        bias = rel_bias[t5_causal_bucket(jnp.maximum(dist, 0))]
        logits = logits + jnp.transpose(bias, (0, 1, 3, 2)).astype(jnp.float32)
        logits = jnp.where((dist >= 0)[:, :, None, :], logits, -jnp.inf)
        p = jax.nn.softmax(logits, axis=-1).astype(v.dtype)
        return jnp.einsum('bqhk,bqkhd->bqhd', p, v_sel)

    out = lax.map(block, (to_blocks(q), to_blocks(q_idx), to_blocks(w_idx),
                          key_pos.reshape(nb, QBLOCK)))
    return jnp.moveaxis(out, 0, 1).reshape(bsz, seq, ATTN_WIDTH)


def setup_inputs(seed: int = 0) -> dict:
    key = jax.random.key(seed)
    ks = jax.random.split(key, 24)
    f32 = jnp.float32
    x = jax.random.normal(ks[0], (BATCH, SEQ, D_MODEL), f32)
    c = jax.random.normal(ks[1], (BATCH, D_MODEL), f32)
    rel_bias = 0.5 * jax.random.normal(ks[2], (NUM_BUCKETS, N_HEADS), f32)
    norm_g = 1.0 + 0.05 * jax.random.normal(ks[3], (DEPTH, D_MODEL), f32)
    w_ada = 0.5 * D_MODEL ** -0.5 * jax.random.normal(ks[4], (DEPTH, D_MODEL, 3 * D_MODEL), f32)
    b_ada = 0.01 * jax.random.normal(ks[5], (DEPTH, 3 * D_MODEL), f32)
    w_in = D_MODEL ** -0.5 * jax.random.normal(ks[6], (DEPTH, D_MODEL, IN_WIDTH), f32)
    q_gain = 1.0 + 0.05 * jax.random.normal(ks[7], (DEPTH, HEAD_DIM), f32)
    k_gain = 1.0 + 0.05 * jax.random.normal(ks[8], (DEPTH, HEAD_DIM), f32)
    n = jnp.arange(SSM_STATE, dtype=f32)
    a_re = -0.5 + 0.01 * jax.random.normal(ks[9], (DEPTH, SSM_GROUPS, SSM_STATE), f32)
    a_im = math.pi * n + 0.01 * jax.random.normal(ks[10], (DEPTH, SSM_GROUPS, SSM_STATE), f32)
    log_dt = jax.random.uniform(ks[11], (DEPTH, SSM_GROUPS), f32,
                                minval=math.log(DT_MIN), maxval=math.log(DT_MAX))
    b_re = (2 * SSM_GROUP) ** -0.5 * jax.random.normal(ks[12], (DEPTH, SSM_GROUPS, SSM_STATE, SSM_GROUP), f32)
    b_im = (2 * SSM_GROUP) ** -0.5 * jax.random.normal(ks[13], (DEPTH, SSM_GROUPS, SSM_STATE, SSM_GROUP), f32)
    c_re = (2 * SSM_STATE) ** -0.5 * jax.random.normal(ks[14], (DEPTH, SSM_GROUPS, SSM_GROUP, SSM_STATE), f32)
    c_im = (2 * SSM_STATE) ** -0.5 * jax.random.normal(ks[15], (DEPTH, SSM_GROUPS, SSM_GROUP, SSM_STATE), f32)
    d_skip = jax.random.normal(ks[16], (DEPTH, SSM_WIDTH), f32)
    w_glu = SSM_WIDTH ** -0.5 * jax.random.normal(ks[17], (DEPTH, SSM_WIDTH, SSM_WIDTH), f32)
    b_glu = 0.01 * jax.random.normal(ks[18], (DEPTH, SSM_WIDTH), f32)
    w_out = MIX_WIDTH ** -0.5 * jax.random.normal(ks[19], (DEPTH, MIX_WIDTH, D_MODEL), f32)
    return {"x": x, "c": c, "rel_bias": rel_bias, "norm_g": norm_g, "w_ada": w_ada,
            "b_ada": b_ada, "w_in": w_in, "q_gain": q_gain, "k_gain": k_gain,
            "a_re": a_re, "a_im": a_im, "log_dt": log_dt, "b_re": b_re, "b_im": b_im,
            "c_re": c_re, "c_im": c_im, "d_skip": d_skip, "w_glu": w_glu, "b_glu": b_glu,
            "w_out": w_out}


def reference(x, c, rel_bias, norm_g, w_ada, b_ada, w_in, q_gain, k_gain, a_re, a_im,
              log_dt, b_re, b_im, c_re, c_im, d_skip, w_glu, b_glu, w_out):
    split_points = np.cumsum(SPLITS)[:-1].tolist()
    cond = jax.nn.silu(c)
    for l in range(DEPTH):
        mod = cond @ w_ada[l] + b_ada[l]
        shift, scale, gate = jnp.split(mod, 3, axis=-1)
        h = rms_norm(x, norm_g[l]) * (1.0 + scale[:, None, :]) + shift[:, None, :]
        proj = h @ w_in[l]
        ssm_u, ssm_z, q, k, v, attn_z, q_idx, k_idx, w_idx = jnp.split(proj, split_points, axis=-1)
        y_ssm = s5_mixer(ssm_u, a_re[l], a_im[l], log_dt[l], b_re[l], b_im[l], c_re[l], c_im[l],
                         d_skip[l], w_glu[l], b_glu[l]) * jax.nn.silu(ssm_z)
        y_attn = dsa_mixer(q, k, v, q_idx, k_idx, w_idx, q_gain[l], k_gain[l], rel_bias) * jax.nn.silu(attn_z)
        y = jnp.concatenate([y_ssm, y_attn], axis=-1) @ w_out[l]
        x = x + gate[:, None, :] * y
    return x
```

```python
import functools
import math

import jax
import jax.numpy as jnp
import numpy as np
from jax import lax
from jax.experimental import pallas as pl
from jax.experimental.pallas import tpu as pltpu

F32 = jnp.float32
BF16 = jnp.bfloat16

SSM_GROUP = 16
SSM_STATE = 64
HEAD_DIM = 64
IDX_HEADS = 8
IDX_DIM = 64
TOPK_MAX = 256
NUM_BUCKETS = 32
MAX_DISTANCE = 128
EPS = 1e-6

LANES = 128
V7X_VMEM_LIMIT_BYTES = 56 * 1024 * 1024

SSM_CHUNK = 64
SSM_TILE = 256
SSM_BLOCK_CH = 128
PROJ_TILE = 512
ATT_TILE = 256
OUT_TILE = 512
BISECT_ITERS = 26
NEG = -1e30

NT_DIMS = (((1,), (1,)), ((), ()))


def _cparams(sem):
    return pltpu.CompilerParams(dimension_semantics=sem,
                                vmem_limit_bytes=V7X_VMEM_LIMIT_BYTES)


def _sigmoid(x):
    return 1.0 / (1.0 + jnp.exp(-x))


def _silu(x):
    return x * _sigmoid(x)


def _ssm_prep_kernel(are_r, aim_r, ldt_r, are_c, aim_c, ldt_c, bre, bim,
                     ep_re, ep_im, em_re, em_im, laml_re, laml_im, bb_re, bb_im):
    dt = jnp.exp(ldt_r[...])
    ar = are_r[...] * dt
    ai = aim_r[...] * dt
    n = ar.shape[-1]
    tau = lax.broadcasted_iota(jnp.int32, (SSM_CHUNK, n), 0).astype(F32)
    mag_p = jnp.exp(tau * ar)
    mag_m = jnp.exp(-tau * ar)
    cs = jnp.cos(tau * ai)
    sn = jnp.sin(tau * ai)
    ep_re[...] = mag_p * cs
    ep_im[...] = mag_p * sn
    em_re[...] = mag_m * cs
    em_im[...] = -(mag_m * sn)
    mag_l = jnp.exp(float(SSM_CHUNK) * ar)
    laml_re[...] = mag_l * jnp.cos(float(SSM_CHUNK) * ai)
    laml_im[...] = mag_l * jnp.sin(float(SSM_CHUNK) * ai)
    a = are_c[...]
    b = aim_c[...]
    dtc = jnp.exp(ldt_c[...])
    mag = jnp.exp(a * dtc)
    x = mag * jnp.cos(b * dtc) - 1.0
    y = mag * jnp.sin(b * dtc)
    den = a * a + b * b
    cre = (x * a + y * b) / den
    cim = (y * a - x * b) / den
    bb_re[...] = cre * bre[...] - cim * bim[...]
    bb_im[...] = cre * bim[...] + cim * bre[...]


def _ssm_prep(a_re, a_im, log_dt, b_re, b_im):
    g, p = a_re.shape
    n = g * p
    ldt = jnp.broadcast_to(log_dt[:, None], (g, p))
    row = lambda t: t.reshape(1, n)
    col = lambda t: t.reshape(n, 1)
    tab = jax.ShapeDtypeStruct((SSM_CHUNK, n), F32)
    vec = jax.ShapeDtypeStruct((1, n), F32)
    bsh = jax.ShapeDtypeStruct((n, SSM_GROUP), F32)
    return pl.pallas_call(
        _ssm_prep_kernel,
        out_shape=(tab, tab, tab, tab, vec, vec, bsh, bsh),
        name="ssm_prep",
    )(row(a_re), row(a_im), row(ldt), col(a_re), col(a_im), col(ldt),
      b_re.reshape(n, SSM_GROUP), b_im.reshape(n, SSM_GROUP))


def _bias_prep_kernel(rb_ref, o_ref):
    t = o_ref.shape[-1]
    n_heads = o_ref.shape[1]
    i = lax.broadcasted_iota(jnp.int32, (t, t), 0)
    j = lax.broadcasted_iota(jnp.int32, (t, t), 1)
    max_exact = NUM_BUCKETS // 2
    for idx in range(3):
        dist = jnp.maximum(idx * t + j - i, 0)
        d = jnp.maximum(dist, 1).astype(F32)
        large = max_exact + (jnp.log(d / max_exact) / math.log(MAX_DISTANCE / max_exact)
                             * (NUM_BUCKETS - max_exact)).astype(jnp.int32)
        large = jnp.minimum(large, NUM_BUCKETS - 1)
        bucket = jnp.where(dist < max_exact, dist, large)
        for h in range(n_heads):
            acc = jnp.zeros((t, t), F32)
            for b in range(NUM_BUCKETS):
                acc = jnp.where(bucket == b, rb_ref[b, h], acc)
            o_ref[idx, h] = acc


def _bias_prep(rel_bias, n_heads):
    t = ATT_TILE
    return pl.pallas_call(
        _bias_prep_kernel,
        out_shape=jax.ShapeDtypeStruct((3, n_heads, t, t), F32),
        in_specs=[pl.BlockSpec(memory_space=pltpu.SMEM)],
        name="bias_prep",
    )(rel_bias)


def _mod_kernel(c_ref, w_ref, b_ref, o_ref):
    c = c_ref[...]
    cond = _silu(c)
    o_ref[...] = jnp.dot(cond.astype(BF16), w_ref[...].astype(BF16),
                         preferred_element_type=F32) + b_ref[...]


def _adaln_mod(c, w_ada, b_ada):
    bsz, d = c.shape
    n = w_ada.shape[1]
    tn = 512
    return pl.pallas_call(
        _mod_kernel,
        out_shape=jax.ShapeDtypeStruct((bsz, n), F32),
        grid=(n // tn,),
        in_specs=[pl.BlockSpec((bsz, d), lambda i: (0, 0)),
                  pl.BlockSpec((d, tn), lambda i: (0, i)),
                  pl.BlockSpec((1, tn), lambda i: (0, i))],
        out_specs=pl.BlockSpec((bsz, tn), lambda i: (0, i)),
        compiler_params=_cparams(("arbitrary",)),
        name="adaln_mod",
    )(c, w_ada, b_ada.reshape(1, n))


def _in_proj_kernel(x_ref, mod_ref, g_ref, wnat_ref, wt_ref, ww_ref, mblk_ref,
                    kg_ref, qg_ref,
                    u_ref, zs_ref, k_ref, kidx_ref, qt_ref, vt_ref, qit_ref, azt_ref, wt_out_ref,
                    *, d_model, width, n_heads):
    x = x_ref[0]
    shift = mod_ref[0, :, 0:d_model]
    scale = mod_ref[0, :, d_model:2 * d_model]
    ms = jnp.mean(x * x, axis=-1, keepdims=True)
    xn = x * lax.rsqrt(ms + EPS) * g_ref[...]
    h = (xn * (1.0 + scale) + shift).astype(BF16)
    tm = h.shape[0]

    nat = jnp.dot(h, wnat_ref[...], preferred_element_type=F32)
    u_ref[0] = nat[:, 0:width].astype(BF16)
    zs_ref[0] = nat[:, width:2 * width].astype(BF16)
    k = nat[:, 2 * width:3 * width]
    kms = jnp.dot((k * k).astype(BF16), mblk_ref[...], preferred_element_type=F32)
    k_ref[0] = (k * lax.rsqrt(kms + EPS) * kg_ref[...]).astype(BF16)
    kidx_ref[0] = nat[:, 3 * width:3 * width + LANES].astype(BF16)

    tr = lax.dot_general(wt_ref[...], h, NT_DIMS, preferred_element_type=F32)
    q3 = tr[0:width].reshape(n_heads, HEAD_DIM, tm)
    qms = jnp.mean(q3 * q3, axis=1, keepdims=True)
    qn = q3 * lax.rsqrt(qms + EPS) * qg_ref[...][None] * (HEAD_DIM ** -0.5)
    qt_ref[0] = qn.reshape(width, tm).astype(BF16)
    vt_ref[0] = tr[width:2 * width].astype(BF16)
    qit_ref[0] = tr[2 * width:3 * width].astype(BF16)
    azt_ref[0] = tr[3 * width:4 * width].astype(BF16)
    wt_out_ref[0] = lax.dot_general(ww_ref[...], h, NT_DIMS,
                                    preferred_element_type=F32) * (IDX_HEADS ** -0.5)


def _in_proj(x, mod, norm_g, w_in, q_gain, k_gain):
    bsz, seq, d = x.shape
    width = d // 2
    n_heads = width // HEAD_DIM
    tm = min(PROJ_TILE, seq)
    o = np.cumsum([0, width, width, width, width, width, width,
                   IDX_HEADS * IDX_DIM, IDX_DIM, IDX_HEADS])
    col = lambda i: w_in[:, int(o[i]):int(o[i + 1])]
    w_kidx = jnp.pad(col(7), ((0, 0), (0, LANES - IDX_DIM)))
    w_nat = jnp.concatenate([col(0), col(1), col(3), w_kidx], axis=1).astype(BF16)
    w_t = jnp.concatenate([col(2), col(4), col(6), col(5)], axis=1).T.astype(BF16)
    w_w = col(8).T.astype(BF16)
    hid = np.arange(width) // HEAD_DIM
    mblk = jnp.asarray((hid[:, None] == hid[None, :]).astype(np.float32) / HEAD_DIM, BF16)
    kg = jnp.tile(k_gain, n_heads).reshape(1, width)
    qg = q_gain.reshape(HEAD_DIM, 1)

    nat_w = w_nat.shape[1]
    bs = jax.ShapeDtypeStruct
    const = lambda shape: pl.BlockSpec(shape, lambda b, s: (0,) * len(shape))
    tok = lambda w: pl.BlockSpec((1, tm, w), lambda b, s: (b, s, 0))
    trn = lambda r: pl.BlockSpec((1, r, tm), lambda b, s: (b, 0, s))
    kern = functools.partial(_in_proj_kernel, d_model=d, width=width, n_heads=n_heads)
    return pl.pallas_call(
        kern,
        out_shape=(bs((bsz, seq, width), BF16), bs((bsz, seq, width), BF16),
                   bs((bsz, seq, width), BF16), bs((bsz, seq, LANES), BF16),
                   bs((bsz, width, seq), BF16), bs((bsz, width, seq), BF16),
                   bs((bsz, width, seq), BF16), bs((bsz, width, seq), BF16),
                   bs((bsz, IDX_HEADS, seq), F32)),
        grid=(bsz, seq // tm),
        in_specs=[tok(d),
                  pl.BlockSpec((1, 1, 3 * d), lambda b, s: (b, 0, 0)),
                  const((1, d)), const((d, nat_w)), const((4 * width, d)),
                  const((IDX_HEADS, d)), const((width, width)),
                  const((1, width)), const((HEAD_DIM, 1))],
        out_specs=(tok(width), tok(width), tok(width), tok(LANES),
                   trn(width), trn(width), trn(width), trn(width), trn(IDX_HEADS)),
        compiler_params=_cparams(("parallel", "arbitrary")),
        name="in_proj",
    )(x, mod.reshape(bsz, 1, 3 * d), norm_g.reshape(1, d), w_nat, w_t, w_w, mblk, kg, qg)


def _s5_kernel(u_ref, zs_ref, bblk_ref, cre_ref, cim_ref, ep_re, ep_im, em_re, em_im,
               laml_re, laml_im, tri_ref, dsk_ref, wglu_ref, bglu_ref,
               o_ref, car_re, car_im, *, n_blocks):
    tm = u_ref.shape[1]
    nch = tm // SSM_CHUNK
    nst = cre_ref.shape[1]

    @pl.when(pl.program_id(1) == 0)
    def _():
        car_re[...] = jnp.zeros_like(car_re)
        car_im[...] = jnp.zeros_like(car_im)

    u = u_ref[0]
    tri = tri_ref[...]
    ys = []
    for j in range(n_blocks):
        st = slice(j * nst, (j + 1) * nst)
        uj = u[:, j * SSM_BLOCK_CH:(j + 1) * SSM_BLOCK_CH]
        p = jnp.dot(uj, bblk_ref[j], preferred_element_type=F32)
        p_re = p[:, :nst].reshape(nch, SSM_CHUNK, nst)
        p_im = p[:, nst:].reshape(nch, SSM_CHUNK, nst)
        emr = em_re[:, st][None]
        emi = em_im[:, st][None]
        w_re = (p_re * emr - p_im * emi).reshape(tm, nst).astype(BF16)
        w_im = (p_re * emi + p_im * emr).reshape(tm, nst).astype(BF16)
        cum_re = jnp.dot(tri, w_re, preferred_element_type=F32).reshape(nch, SSM_CHUNK, nst)
        cum_im = jnp.dot(tri, w_im, preferred_element_type=F32).reshape(nch, SSM_CHUNK, nst)
        lr = laml_re[:, st]
        li = laml_im[:, st]
        cr = car_re[:, st]
        ci = car_im[:, st]
        crs, cis = [], []
        for c in range(nch):
            crs.append(cr)
            cis.append(ci)
            tr_ = cum_re[c, SSM_CHUNK - 1:SSM_CHUNK, :] + cr
            ti_ = cum_im[c, SSM_CHUNK - 1:SSM_CHUNK, :] + ci
            cr = lr * tr_ - li * ti_
            ci = lr * ti_ + li * tr_
        car_re[:, st] = cr
        car_im[:, st] = ci
        a_re = cum_re + jnp.stack(crs, axis=0)
        a_im = cum_im + jnp.stack(cis, axis=0)
        epr = ep_re[:, st][None]
        epi = ep_im[:, st][None]
        x_re = (a_re * epr - a_im * epi).reshape(tm, nst).astype(BF16)
        x_im = (a_re * epi + a_im * epr).reshape(tm, nst).astype(BF16)
        ys.append(jnp.dot(x_re, cre_ref[j], preferred_element_type=F32)
                  - jnp.dot(x_im, cim_ref[j], preferred_element_type=F32))
    y = jnp.concatenate(ys, axis=1) + dsk_ref[...] * u.astype(F32)
    z = 0.5 * y * (1.0 + jnp.tanh(math.sqrt(2.0 / math.pi) * (y + 0.044715 * (y * y * y))))
    gl = jnp.dot(z.astype(BF16), wglu_ref[...], preferred_element_type=F32) + bglu_ref[...]
    zz = z * _sigmoid(gl)
    o_ref[0] = (zz * _silu(zs_ref[0].astype(F32))).astype(BF16)


def _s5_mixer(u, zs, tabs, c_re, c_im, d_skip, w_glu, b_glu):
    bsz, seq, width = u.shape
    ep_re, ep_im, em_re, em_im, laml_re, laml_im, bb_re, bb_im = tabs
    tm = min(SSM_TILE, seq)
    gpb = SSM_BLOCK_CH // SSM_GROUP
    n_blocks = width // SSM_BLOCK_CH
    nst = gpb * SSM_STATE
    eye = jnp.eye(gpb, dtype=F32)

    def b_block(t):
        t4 = t.reshape(n_blocks, gpb, SSM_STATE, SSM_GROUP)
        return jnp.einsum('jgpc,gh->jgchp', t4, eye).reshape(n_blocks, SSM_BLOCK_CH, nst)

    def c_block(t):
        t4 = t.reshape(n_blocks, gpb, SSM_GROUP, SSM_STATE)
        return jnp.einsum('jgcp,gh->jhpgc', t4, eye).reshape(n_blocks, nst, SSM_BLOCK_CH)

    bblk = jnp.concatenate([b_block(bb_re), b_block(bb_im)], axis=2).astype(BF16)
    cre = c_block(c_re).astype(BF16)
    cim = c_block(c_im).astype(BF16)
    t_idx = np.arange(tm)
    tri = jnp.asarray(((t_idx[:, None] >= t_idx[None, :])
                       & (t_idx[:, None] // SSM_CHUNK == t_idx[None, :] // SSM_CHUNK)
                       ).astype(np.float32), BF16)
    n = ep_re.shape[1]
    const = lambda shape: pl.BlockSpec(shape, lambda b, s: (0,) * len(shape))
    tok = pl.BlockSpec((1, tm, width), lambda b, s: (b, s, 0))
    kern = functools.partial(_s5_kernel, n_blocks=n_blocks)
    return pl.pallas_call(
        kern,
        out_shape=jax.ShapeDtypeStruct((bsz, seq, width), BF16),
        grid=(bsz, seq // tm),
        in_specs=[tok, tok,
                  const((n_blocks, SSM_BLOCK_CH, 2 * nst)),
                  const((n_blocks, nst, SSM_BLOCK_CH)), const((n_blocks, nst, SSM_BLOCK_CH)),
                  const((SSM_CHUNK, n)), const((SSM_CHUNK, n)),
                  const((SSM_CHUNK, n)), const((SSM_CHUNK, n)),
                  const((1, n)), const((1, n)), const((tm, tm)),
                  const((1, width)), const((width, width)), const((1, width))],
        out_specs=tok,
        scratch_shapes=[pltpu.VMEM((1, n), F32), pltpu.VMEM((1, n), F32)],
        compiler_params=_cparams(("parallel", "arbitrary")),
        name="s5_mixer",
    )(u, zs, bblk, cre, cim, ep_re, ep_im, em_re, em_im, laml_re, laml_im, tri,
      d_skip.reshape(1, width), w_glu.astype(BF16), b_glu.reshape(1, width))


def _dsa_kernel(k_ref, vt_ref, kidx_ref, qt_ref, qit_ref, azt_ref, wt_ref, tz_ref,
                o_ref, sc_ref, m_ref, l_ref, acc_ref, ot_ref, *, topk, n_heads):
    t = ATT_TILE
    qi = pl.program_id(1)
    nk = qi + 1
    seq = k_ref.shape[1]
    row = lax.broadcasted_iota(jnp.int32, (t, t), 0)
    colq = lax.broadcasted_iota(jnp.int32, (t, t), 1)
    lane = lax.broadcasted_iota(jnp.int32, (1, t), 1)
    pos = qi * t + lane
    kq = jnp.minimum(pos + 1, topk).astype(F32)
    prow = lax.broadcasted_iota(jnp.int32, (2 * HEAD_DIM, t), 0)

    def tile_rows(kt):
        return pl.ds(pl.multiple_of(kt * t, t), t)

    qpads = []
    for h in range(IDX_HEADS):
        qh = qit_ref[0, h * IDX_DIM:(h + 1) * IDX_DIM, :]
        qpads.append(jnp.concatenate([qh, jnp.zeros_like(qh)], axis=0))
    wts = [wt_ref[0, h:h + 1, :] for h in range(IDX_HEADS)]

    def score_tile(kt, carry):
        smin, smax = carry
        kx = kidx_ref[0, tile_rows(kt), :]
        s = jnp.zeros((t, t), F32)
        for h in range(IDX_HEADS):
            rel = jnp.dot(kx, qpads[h], preferred_element_type=F32)
            s = s + wts[h] * jnp.maximum(rel, 0.0)
        causal = (kt * t + row) <= (qi * t + colq)
        sc_ref[tile_rows(kt), :] = jnp.where(causal, s, -jnp.inf)
        smin = jnp.minimum(smin, jnp.min(jnp.where(causal, s, jnp.inf), axis=0, keepdims=True))
        smax = jnp.maximum(smax, jnp.max(jnp.where(causal, s, -jnp.inf), axis=0, keepdims=True))
        return smin, smax

    smin, smax = lax.fori_loop(
        0, nk, score_tile,
        (jnp.full((1, t), jnp.inf, F32), jnp.full((1, t), -jnp.inf, F32)))

    def count_ge(thr):
        def body(kt, acc):
            m = sc_ref[tile_rows(kt), :] >= thr
            return acc + jnp.sum(jnp.where(m, 1.0, 0.0).reshape(t // 8, 8, t), axis=0)
        acc = lax.fori_loop(0, nk, body, jnp.zeros((8, t), F32))
        return jnp.sum(acc, axis=0, keepdims=True)

    def count_gt(thr):
        def body(kt, acc):
            m = sc_ref[tile_rows(kt), :] > thr
            return acc + jnp.sum(jnp.where(m, 1.0, 0.0).reshape(t // 8, 8, t), axis=0)
        acc = lax.fori_loop(0, nk, body, jnp.zeros((8, t), F32))
        return jnp.sum(acc, axis=0, keepdims=True)

    def min_where(pred_fn):
        def body(kt, acc):
            s = sc_ref[tile_rows(kt), :]
            return jnp.minimum(acc, jnp.min(jnp.where(pred_fn(s), s, jnp.inf), axis=0, keepdims=True))
        return lax.fori_loop(0, nk, body, jnp.full((1, t), jnp.inf, F32))

    def bisect(_, st):
        lo, hi, c_lo = st
        mid = lo + 0.5 * (hi - lo)
        cnt = count_ge(mid)
        up = cnt >= kq
        return jnp.where(up, mid, lo), jnp.where(up, hi, mid), jnp.where(up, cnt, c_lo)

    lo, _, c_lo = lax.fori_loop(0, BISECT_ITERS, bisect,
                                (smin, smax, (pos + 1).astype(F32)))
    surplus = jnp.max(c_lo - kq)

    @pl.when(surplus <= 0.0)
    def _():
        def body(kt, c):
            s = sc_ref[tile_rows(kt), :]
            sc_ref[tile_rows(kt), :] = jnp.where(s >= lo, 0.0, NEG)
            return c
        lax.fori_loop(0, nk, body, 0)

    @pl.when(surplus > 0.0)
    def _():
        def peel_cond(st):
            return st[3] > 0.0

        def peel(st):
            lo_, c_lo_, _, _ = st
            taup = min_where(lambda s: s >= lo_)
            n_gt = count_gt(taup)
            need_peel = (c_lo_ != kq) & (n_gt >= kq)
            nxt = min_where(lambda s: s > taup)
            lo_n = jnp.where(need_peel, nxt, lo_)
            c_n = jnp.where(need_peel, n_gt, c_lo_)
            n_gt_keep = jnp.where(need_peel, 0.0, n_gt)
            return lo_n, c_n, n_gt_keep, jnp.max(jnp.where(need_peel, 1.0, 0.0))

        lo2, c2, _, _ = lax.while_loop(
            peel_cond, peel, (lo, c_lo, jnp.zeros((1, t), F32), jnp.float32(1.0)))
        taup = min_where(lambda s: s >= lo2)
        n_gt = count_gt(taup)
        tie = c2 != kq
        need = kq - n_gt

        def count_tie_le(cut):
            def body(kt, acc):
                s = sc_ref[tile_rows(kt), :]
                kidx = (kt * t + row).astype(F32)
                m = (s == taup) & (kidx <= cut)
                return acc + jnp.sum(jnp.where(m, 1.0, 0.0).reshape(t // 8, 8, t), axis=0)
            acc = lax.fori_loop(0, nk, body, jnp.zeros((8, t), F32))
            return jnp.sum(acc, axis=0, keepdims=True)

        def ibisect(_, st):
            lo_i, hi_i = st
            mid_i = jnp.floor(0.5 * (lo_i + hi_i))
            ok = count_tie_le(mid_i) >= need
            return jnp.where(ok, lo_i, mid_i), jnp.where(ok, mid_i, hi_i)

        n_it = int(math.ceil(math.log2(seq))) + 1
        _, cut = lax.fori_loop(0, n_it, ibisect,
                               (jnp.full((1, t), -1.0, F32), jnp.full((1, t), float(seq - 1), F32)))
        cut = jnp.where(tie, cut, float(seq))

        def body(kt, c):
            s = sc_ref[tile_rows(kt), :]
            kidx = (kt * t + row).astype(F32)
            keep = (s >= lo2) & jnp.logical_not((s == taup) & (kidx > cut))
            sc_ref[tile_rows(kt), :] = jnp.where(keep, 0.0, NEG)
            return c
        lax.fori_loop(0, nk, body, 0)

    def head_body(h, c):
        hs = pl.ds(pl.multiple_of(h * HEAD_DIM, HEAD_DIM), HEAD_DIM)
        pair = pl.ds(pl.multiple_of((h // 2) * 2 * HEAD_DIM, 2 * HEAD_DIM), 2 * HEAD_DIM)
        qh = qt_ref[0, hs, :]
        q2 = jnp.concatenate([qh, qh], axis=0)
        keep = (prow < HEAD_DIM) == (jnp.bitwise_and(h, 1) == 0)
        qpad = jnp.where(keep, q2, jnp.zeros_like(q2))
        m_ref[...] = jnp.full_like(m_ref, NEG)
        l_ref[...] = jnp.zeros_like(l_ref)
        acc_ref[...] = jnp.zeros_like(acc_ref)

        def kv_tile(kt, c2):
            kk = k_ref[0, tile_rows(kt), pair]
            s = jnp.dot(kk, qpad, preferred_element_type=F32)
            near = jnp.minimum(qi - kt, 2)
            s = s + tz_ref[near, h] + sc_ref[tile_rows(kt), :]
            m_old = m_ref[...]
            m_new = jnp.maximum(m_old, jnp.max(s, axis=0, keepdims=True))
            corr = jnp.exp(m_old - m_new)
            p = jnp.exp(s - m_new)
            l_ref[...] = l_ref[...] * corr + jnp.sum(p, axis=0, keepdims=True)
            vv = vt_ref[0, hs, tile_rows(kt)]
            acc_ref[...] = acc_ref[...] * corr + jnp.dot(vv, p.astype(BF16),
                                                          preferred_element_type=F32)
            m_ref[...] = m_new
            return c2

        lax.fori_loop(0, nk, kv_tile, 0)
        out = acc_ref[...] / l_ref[...]
        ot_ref[hs, :] = out * _silu(azt_ref[0, hs, :].astype(F32))
        return c

    lax.fori_loop(0, n_heads, head_body, 0)
    o_ref[0] = ot_ref[...].T.astype(BF16)


def _dsa_mixer(k, vt, kidx, qt, qit, azt, wt, tz, topk):
    bsz, seq, width = k.shape
    n_heads = width // HEAD_DIM
    t = ATT_TILE
    per_b = lambda shape: pl.BlockSpec((1,) + shape, lambda b, q: (b, 0, 0))
    qtile = lambda r: pl.BlockSpec((1, r, t), lambda b, q: (b, 0, q))
    kern = functools.partial(_dsa_kernel, topk=topk, n_heads=n_heads)
    return pl.pallas_call(
        kern,
        out_shape=jax.ShapeDtypeStruct((bsz, seq, width), BF16),
        grid=(bsz, seq // t),
        in_specs=[per_b((seq, width)), per_b((width, seq)), per_b((seq, LANES)),
                  qtile(width), qtile(width), qtile(width), qtile(IDX_HEADS),
                  pl.BlockSpec((3, n_heads, t, t), lambda b, q: (0, 0, 0, 0))],
        out_specs=pl.BlockSpec((1, t, width), lambda b, q: (b, q, 0)),
        scratch_shapes=[pltpu.VMEM((seq, t), F32), pltpu.VMEM((1, t), F32),
                        pltpu.VMEM((1, t), F32), pltpu.VMEM((HEAD_DIM, t), F32),
                        pltpu.VMEM((width, t), F32)],
        compiler_params=_cparams(("parallel", "arbitrary")),
        name="dsa_mixer",
    )(k, vt, kidx, qt, qit, azt, wt, tz)


def _out_kernel(x_ref, ys_ref, ya_ref, w1_ref, w2_ref, mod_ref, o_ref, *, d_model):
    gate = mod_ref[0, :, 2 * d_model:3 * d_model]
    y = (jnp.dot(ys_ref[0], w1_ref[...], preferred_element_type=F32)
         + jnp.dot(ya_ref[0], w2_ref[...], preferred_element_type=F32))
    o_ref[0] = x_ref[0] + gate * y


def _out_proj(x, y_ssm, y_attn, w_out, mod):
    bsz, seq, d = x.shape
    width = y_ssm.shape[-1]
    tm = min(OUT_TILE, seq)
    tok = lambda w: pl.BlockSpec((1, tm, w), lambda b, s: (b, s, 0))
    const = lambda shape: pl.BlockSpec(shape, lambda b, s: (0,) * len(shape))
    return pl.pallas_call(
        functools.partial(_out_kernel, d_model=d),
        out_shape=jax.ShapeDtypeStruct((bsz, seq, d), x.dtype),
        grid=(bsz, seq // tm),
        in_specs=[tok(d), tok(width), tok(width), const((width, d)), const((width, d)),
                  pl.BlockSpec((1, 1, 3 * d), lambda b, s: (b, 0, 0))],
        out_specs=tok(d),
        compiler_params=_cparams(("parallel", "arbitrary")),
        name="out_proj",
    )(x, y_ssm, y_attn, w_out[:width].astype(BF16), w_out[width:].astype(BF16),
      mod.reshape(bsz, 1, 3 * d))


def kernel(x, c, rel_bias, norm_g, w_ada, b_ada, w_in, q_gain, k_gain, a_re, a_im, log_dt,
           b_re, b_im, c_re, c_im, d_skip, w_glu, b_glu, w_out):
    bsz, seq, d = x.shape
    depth = w_in.shape[0]
    width = d // 2
    n_heads = width // HEAD_DIM
    topk = min(TOPK_MAX, seq // 4)
    assert seq % ATT_TILE == 0 and seq % SSM_CHUNK == 0
    assert ATT_TILE + 1 >= MAX_DISTANCE
    tz = _bias_prep(rel_bias, n_heads)
    for l in range(depth):
        mod = _adaln_mod(c, w_ada[l], b_ada[l])
        u, zs, k, kidx, qt, vt, qit, azt, wt = _in_proj(x, mod, norm_g[l], w_in[l],
                                                        q_gain[l], k_gain[l])
        tabs = _ssm_prep(a_re[l], a_im[l], log_dt[l], b_re[l], b_im[l])
        y_ssm = _s5_mixer(u, zs, tabs, c_re[l], c_im[l], d_skip[l], w_glu[l], b_glu[l])
        y_attn = _dsa_mixer(k, vt, kidx, qt, qit, azt, wt, tz, topk)
        x = _out_proj(x, y_ssm, y_attn, w_out[l], mod)
    return x
```

```python
import functools
import math

import jax
import jax.numpy as jnp
import numpy as np
from jax import lax
from jax.experimental import pallas as pl
from jax.experimental.pallas import tpu as pltpu

F32 = jnp.float32
BF16 = jnp.bfloat16

SSM_GROUP = 16
SSM_STATE = 64
HEAD_DIM = 64
IDX_HEADS = 8
IDX_DIM = 64
TOPK_MAX = 256
NUM_BUCKETS = 32
MAX_DISTANCE = 128
EPS = 1e-6

LANES = 128
V7X_VMEM_LIMIT_BYTES = 56 * 1024 * 1024

SSM_CHUNK = 64
SSM_TILE = 256
SSM_BLOCK_CH = 128
PROJ_TILE = 512
ATT_TILE = 256
OUT_TILE = 512
BISECT_MIN_ITERS = 12
BISECT_MAX_ITERS = 40
NEG = -1e30
LOG2E = math.log2(math.e)

NT_DIMS = (((1,), (1,)), ((), ()))


def _cparams(sem):
    return pltpu.CompilerParams(dimension_semantics=sem,
                                vmem_limit_bytes=V7X_VMEM_LIMIT_BYTES)


def _sigmoid(x):
    return 1.0 / (1.0 + jnp.exp(-x))


def _silu(x):
    return x * _sigmoid(x)


def _ssm_prep_kernel(are_r, aim_r, ldt_r, are_c, aim_c, ldt_c, bre, bim,
                     ep_re, ep_im, em_re, em_im, laml_re, laml_im, bb_re, bb_im):
    dt = jnp.exp(ldt_r[...])
    ar = are_r[...] * dt
    ai = aim_r[...] * dt
    n = ar.shape[-1]
    tau = lax.broadcasted_iota(jnp.int32, (SSM_CHUNK, n), 0).astype(F32)
    mag_p = jnp.exp(tau * ar)
    mag_m = jnp.exp(-tau * ar)
    cs = jnp.cos(tau * ai)
    sn = jnp.sin(tau * ai)
    ep_re[...] = mag_p * cs
    ep_im[...] = mag_p * sn
    em_re[...] = mag_m * cs
    em_im[...] = -(mag_m * sn)
    mag_l = jnp.exp(float(SSM_CHUNK) * ar)
    laml_re[...] = mag_l * jnp.cos(float(SSM_CHUNK) * ai)
    laml_im[...] = mag_l * jnp.sin(float(SSM_CHUNK) * ai)
    a = are_c[...]
    b = aim_c[...]
    dtc = jnp.exp(ldt_c[...])
    mag = jnp.exp(a * dtc)
    x = mag * jnp.cos(b * dtc) - 1.0
    y = mag * jnp.sin(b * dtc)
    den = a * a + b * b
    cre = (x * a + y * b) / den
    cim = (y * a - x * b) / den
    bb_re[...] = cre * bre[...] - cim * bim[...]
    bb_im[...] = cre * bim[...] + cim * bre[...]


def _ssm_prep(a_re, a_im, log_dt, b_re, b_im):
    g, p = a_re.shape
    n = g * p
    ldt = jnp.broadcast_to(log_dt[:, None], (g, p))
    row = lambda t: t.reshape(1, n)
    col = lambda t: t.reshape(n, 1)
    tab = jax.ShapeDtypeStruct((SSM_CHUNK, n), F32)
    vec = jax.ShapeDtypeStruct((1, n), F32)
    bsh = jax.ShapeDtypeStruct((n, SSM_GROUP), F32)
    return pl.pallas_call(
        _ssm_prep_kernel,
        out_shape=(tab, tab, tab, tab, vec, vec, bsh, bsh),
        name="ssm_prep",
    )(row(a_re), row(a_im), row(ldt), col(a_re), col(a_im), col(ldt),
      b_re.reshape(n, SSM_GROUP), b_im.reshape(n, SSM_GROUP))


def _bias_prep_kernel(rb_ref, o_ref):
    t = o_ref.shape[-1]
    n_heads = o_ref.shape[1]
    i = lax.broadcasted_iota(jnp.int32, (t, t), 0)
    j = lax.broadcasted_iota(jnp.int32, (t, t), 1)
    max_exact = NUM_BUCKETS // 2

    def bias(idx, h):
        dist = jnp.maximum(idx * t + j - i, 0)
        d = jnp.maximum(dist, 1).astype(F32)
        large = max_exact + (jnp.log(d / max_exact) / math.log(MAX_DISTANCE / max_exact)
                             * (NUM_BUCKETS - max_exact)).astype(jnp.int32)
        large = jnp.minimum(large, NUM_BUCKETS - 1)
        bucket = jnp.where(dist < max_exact, dist, large)
        acc = jnp.zeros((t, t), F32)
        for b in range(NUM_BUCKETS):
            acc = jnp.where(bucket == b, rb_ref[b, h], acc)
        return acc

    for h in range(n_heads):
        far = bias(2, h)
        for idx in range(2):
            o_ref[idx, h] = (bias(idx, h) - far) * LOG2E


def _bias_prep(rel_bias, n_heads):
    t = ATT_TILE
    return pl.pallas_call(
        _bias_prep_kernel,
        out_shape=jax.ShapeDtypeStruct((2, n_heads, t, t), F32),
        in_specs=[pl.BlockSpec(memory_space=pltpu.SMEM)],
        name="bias_prep",
    )(rel_bias)


def _mod_kernel(c_ref, w_ref, b_ref, o_ref):
    c = c_ref[...]
    cond = _silu(c)
    o_ref[...] = jnp.dot(cond.astype(BF16), w_ref[...].astype(BF16),
                         preferred_element_type=F32) + b_ref[...]


def _adaln_mod(c, w_ada, b_ada):
    bsz, d = c.shape
    n = w_ada.shape[1]
    tn = 512
    return pl.pallas_call(
        _mod_kernel,
        out_shape=jax.ShapeDtypeStruct((bsz, n), F32),
        grid=(n // tn,),
        in_specs=[pl.BlockSpec((bsz, d), lambda i: (0, 0)),
                  pl.BlockSpec((d, tn), lambda i: (0, i)),
                  pl.BlockSpec((1, tn), lambda i: (0, i))],
        out_specs=pl.BlockSpec((bsz, tn), lambda i: (0, i)),
        compiler_params=_cparams(("arbitrary",)),
        name="adaln_mod",
    )(c, w_ada, b_ada.reshape(1, n))


def _in_proj_kernel(x_ref, mod_ref, g_ref, wnat_ref, wt_ref, ww_ref, mblk_ref,
                    kg_ref, qg_ref,
                    u_ref, zs_ref, k_ref, kidx_ref, qt_ref, vt_ref, qit_ref, azt_ref, wt_out_ref,
                    *, d_model, width, n_heads):
    x = x_ref[0]
    shift = mod_ref[0, :, 0:d_model]
    scale = mod_ref[0, :, d_model:2 * d_model]
    ms = jnp.mean(x * x, axis=-1, keepdims=True)
    xn = x * lax.rsqrt(ms + EPS) * g_ref[...]
    h = (xn * (1.0 + scale) + shift).astype(BF16)
    tm = h.shape[0]

    nat = jnp.dot(h, wnat_ref[...], preferred_element_type=F32)
    u_ref[0] = nat[:, 0:width].astype(BF16)
    zs_ref[0] = nat[:, width:2 * width].astype(BF16)
    k = nat[:, 2 * width:3 * width]
    kms = jnp.dot((k * k).astype(BF16), mblk_ref[...], preferred_element_type=F32)
    k_ref[0] = (k * lax.rsqrt(kms + EPS) * kg_ref[...]).astype(BF16)
    kidx_ref[0] = nat[:, 3 * width:3 * width + LANES].astype(BF16)

    tr = lax.dot_general(wt_ref[...], h, NT_DIMS, preferred_element_type=F32)
    q3 = tr[0:width].reshape(n_heads, HEAD_DIM, tm)
    qms = jnp.mean(q3 * q3, axis=1, keepdims=True)
    qn = q3 * lax.rsqrt(qms + EPS) * qg_ref[...][None] * (HEAD_DIM ** -0.5 * LOG2E)
    qt_ref[0] = qn.reshape(width, tm).astype(BF16)
    vt_ref[0] = tr[width:2 * width].astype(BF16)
    qit_ref[0] = tr[2 * width:3 * width].astype(BF16)
    azt_ref[0] = tr[3 * width:4 * width].astype(BF16)
    wt_out_ref[0] = lax.dot_general(ww_ref[...], h, NT_DIMS,
                                    preferred_element_type=F32) * (IDX_HEADS ** -0.5)


def _in_proj(x, mod, norm_g, w_in, q_gain, k_gain):
    bsz, seq, d = x.shape
    width = d // 2
    n_heads = width // HEAD_DIM
    tm = min(PROJ_TILE, seq)
    o = np.cumsum([0, width, width, width, width, width, width,
                   IDX_HEADS * IDX_DIM, IDX_DIM, IDX_HEADS])
    col = lambda i: w_in[:, int(o[i]):int(o[i + 1])]
    w_kidx = jnp.pad(col(7), ((0, 0), (0, LANES - IDX_DIM)))
    w_nat = jnp.concatenate([col(0), col(1), col(3), w_kidx], axis=1).astype(BF16)
    w_t = jnp.concatenate([col(2), col(4), col(6), col(5)], axis=1).T.astype(BF16)
    w_w = col(8).T.astype(BF16)
    hid = np.arange(width) // HEAD_DIM
    mblk = jnp.asarray((hid[:, None] == hid[None, :]).astype(np.float32) / HEAD_DIM, BF16)
    kg = jnp.tile(k_gain, n_heads).reshape(1, width)
    qg = q_gain.reshape(HEAD_DIM, 1)

    nat_w = w_nat.shape[1]
    bs = jax.ShapeDtypeStruct
    const = lambda shape: pl.BlockSpec(shape, lambda b, s: (0,) * len(shape))
    tok = lambda w: pl.BlockSpec((1, tm, w), lambda b, s: (b, s, 0))
    trn = lambda r: pl.BlockSpec((1, r, tm), lambda b, s: (b, 0, s))
    kern = functools.partial(_in_proj_kernel, d_model=d, width=width, n_heads=n_heads)
    return pl.pallas_call(
        kern,
        out_shape=(bs((bsz, seq, width), BF16), bs((bsz, seq, width), BF16),
                   bs((bsz, seq, width), BF16), bs((bsz, seq, LANES), BF16),
                   bs((bsz, width, seq), BF16), bs((bsz, width, seq), BF16),
                   bs((bsz, width, seq), BF16), bs((bsz, width, seq), BF16),
                   bs((bsz, IDX_HEADS, seq), F32)),
        grid=(bsz, seq // tm),
        in_specs=[tok(d),
                  pl.BlockSpec((1, 1, 3 * d), lambda b, s: (b, 0, 0)),
                  const((1, d)), const((d, nat_w)), const((4 * width, d)),
                  const((IDX_HEADS, d)), const((width, width)),
                  const((1, width)), const((HEAD_DIM, 1))],
        out_specs=(tok(width), tok(width), tok(width), tok(LANES),
                   trn(width), trn(width), trn(width), trn(width), trn(IDX_HEADS)),
        compiler_params=_cparams(("parallel", "arbitrary")),
        name="in_proj",
    )(x, mod.reshape(bsz, 1, 3 * d), norm_g.reshape(1, d), w_nat, w_t, w_w, mblk, kg, qg)


def _s5_kernel(u_ref, zs_ref, bblk_ref, cre_ref, cim_ref, ep_re, ep_im, em_re, em_im,
               laml_re, laml_im, tri_ref, dsk_ref, wglu_ref, bglu_ref,
               o_ref, car_re, car_im, *, n_blocks):
    tm = u_ref.shape[1]
    nch = tm // SSM_CHUNK
    nst = cre_ref.shape[1]

    @pl.when(pl.program_id(1) == 0)
    def _():
        car_re[...] = jnp.zeros_like(car_re)
        car_im[...] = jnp.zeros_like(car_im)

    u = u_ref[0]
    tri = tri_ref[...]
    ys = []
    for j in range(n_blocks):
        st = slice(j * nst, (j + 1) * nst)
        uj = u[:, j * SSM_BLOCK_CH:(j + 1) * SSM_BLOCK_CH]
        p = jnp.dot(uj, bblk_ref[j], preferred_element_type=F32)
        p_re = p[:, :nst].reshape(nch, SSM_CHUNK, nst)
        p_im = p[:, nst:].reshape(nch, SSM_CHUNK, nst)
        emr = em_re[:, st][None]
        emi = em_im[:, st][None]
        w_re = (p_re * emr - p_im * emi).reshape(tm, nst).astype(BF16)
        w_im = (p_re * emi + p_im * emr).reshape(tm, nst).astype(BF16)
        cum_re = jnp.dot(tri, w_re, preferred_element_type=F32).reshape(nch, SSM_CHUNK, nst)
        cum_im = jnp.dot(tri, w_im, preferred_element_type=F32).reshape(nch, SSM_CHUNK, nst)
        lr = laml_re[:, st]
        li = laml_im[:, st]
        cr = car_re[:, st]
        ci = car_im[:, st]
        crs, cis = [], []
        for c in range(nch):
            crs.append(cr)
            cis.append(ci)
            tr_ = cum_re[c, SSM_CHUNK - 1:SSM_CHUNK, :] + cr
            ti_ = cum_im[c, SSM_CHUNK - 1:SSM_CHUNK, :] + ci
            cr = lr * tr_ - li * ti_
            ci = lr * ti_ + li * tr_
        car_re[:, st] = cr
        car_im[:, st] = ci
        a_re = cum_re + jnp.stack(crs, axis=0)
        a_im = cum_im + jnp.stack(cis, axis=0)
        epr = ep_re[:, st][None]
        epi = ep_im[:, st][None]
        x_re = (a_re * epr - a_im * epi).reshape(tm, nst).astype(BF16)
        x_im = (a_re * epi + a_im * epr).reshape(tm, nst).astype(BF16)
        ys.append(jnp.dot(x_re, cre_ref[j], preferred_element_type=F32)
                  - jnp.dot(x_im, cim_ref[j], preferred_element_type=F32))
    y = jnp.concatenate(ys, axis=1) + dsk_ref[...] * u.astype(F32)
    z = 0.5 * y * (1.0 + jnp.tanh(math.sqrt(2.0 / math.pi) * (y + 0.044715 * (y * y * y))))
    gl = jnp.dot(z.astype(BF16), wglu_ref[...], preferred_element_type=F32) + bglu_ref[...]
    zz = z * _sigmoid(gl)
    o_ref[0] = (zz * _silu(zs_ref[0].astype(F32))).astype(BF16)


def _s5_mixer(u, zs, tabs, c_re, c_im, d_skip, w_glu, b_glu):
    bsz, seq, width = u.shape
    ep_re, ep_im, em_re, em_im, laml_re, laml_im, bb_re, bb_im = tabs
    tm = min(SSM_TILE, seq)
    gpb = SSM_BLOCK_CH // SSM_GROUP
    n_blocks = width // SSM_BLOCK_CH
    nst = gpb * SSM_STATE
    eye = jnp.eye(gpb, dtype=F32)

    def b_block(t):
        t4 = t.reshape(n_blocks, gpb, SSM_STATE, SSM_GROUP)
        return jnp.einsum('jgpc,gh->jgchp', t4, eye).reshape(n_blocks, SSM_BLOCK_CH, nst)

    def c_block(t):
        t4 = t.reshape(n_blocks, gpb, SSM_GROUP, SSM_STATE)
        return jnp.einsum('jgcp,gh->jhpgc', t4, eye).reshape(n_blocks, nst, SSM_BLOCK_CH)

    bblk = jnp.concatenate([b_block(bb_re), b_block(bb_im)], axis=2).astype(BF16)
    cre = c_block(c_re).astype(BF16)
    cim = c_block(c_im).astype(BF16)
    t_idx = np.arange(tm)
    tri = jnp.asarray(((t_idx[:, None] >= t_idx[None, :])
                       & (t_idx[:, None] // SSM_CHUNK == t_idx[None, :] // SSM_CHUNK)
                       ).astype(np.float32), BF16)
    n = ep_re.shape[1]
    const = lambda shape: pl.BlockSpec(shape, lambda b, s: (0,) * len(shape))
    tok = pl.BlockSpec((1, tm, width), lambda b, s: (b, s, 0))
    kern = functools.partial(_s5_kernel, n_blocks=n_blocks)
    return pl.pallas_call(
        kern,
        out_shape=jax.ShapeDtypeStruct((bsz, seq, width), BF16),
        grid=(bsz, seq // tm),
        in_specs=[tok, tok,
                  const((n_blocks, SSM_BLOCK_CH, 2 * nst)),
                  const((n_blocks, nst, SSM_BLOCK_CH)), const((n_blocks, nst, SSM_BLOCK_CH)),
                  const((SSM_CHUNK, n)), const((SSM_CHUNK, n)),
                  const((SSM_CHUNK, n)), const((SSM_CHUNK, n)),
                  const((1, n)), const((1, n)), const((tm, tm)),
                  const((1, width)), const((width, width)), const((1, width))],
        out_specs=tok,
        scratch_shapes=[pltpu.VMEM((1, n), F32), pltpu.VMEM((1, n), F32)],
        compiler_params=_cparams(("parallel", "arbitrary")),
        name="s5_mixer",
    )(u, zs, bblk, cre, cim, ep_re, ep_im, em_re, em_im, laml_re, laml_im, tri,
      d_skip.reshape(1, width), w_glu.astype(BF16), b_glu.reshape(1, width))


def _dsa_kernel(k_ref, vt_ref, kidx_ref, qt_ref, qit_ref, azt_ref, wt_ref, tz_ref,
                o_ref, sc_ref, qip_ref, qp_ref, m_ref, l_ref, acc_ref, s_ref, *, topk, n_heads):
    t = ATT_TILE
    qi = pl.program_id(1)
    nk = qi + 1
    seq = k_ref.shape[1]
    row = lax.broadcasted_iota(jnp.int32, (t, t), 0)
    colq = lax.broadcasted_iota(jnp.int32, (t, t), 1)
    lane = lax.broadcasted_iota(jnp.int32, (1, t), 1)
    pos = qi * t + lane
    kq = jnp.minimum(pos + 1, topk).astype(F32)

    def tile_rows(kt):
        return pl.ds(pl.multiple_of(kt * t, t), t)

    zpad = jnp.zeros((HEAD_DIM, t), BF16)
    for h in range(IDX_HEADS):
        qip_ref[h] = jnp.concatenate([qit_ref[0, h * IDX_DIM:(h + 1) * IDX_DIM, :], zpad], axis=0)
    for h in range(n_heads):
        qh = qt_ref[0, h * HEAD_DIM:(h + 1) * HEAD_DIM, :]
        qp_ref[h] = jnp.concatenate([qh, zpad] if h % 2 == 0 else [zpad, qh], axis=0)

    def score_tile(kt):
        kx = kidx_ref[0, tile_rows(kt), :]
        s = jnp.zeros((t, t), F32)
        for h in range(IDX_HEADS):
            rel = jnp.dot(kx, qip_ref[h], preferred_element_type=F32)
            s = s + wt_ref[0, h:h + 1, :] * jnp.maximum(rel, 0.0)
        return s

    def score_far(kt, carry):
        smin, smax = carry
        s = score_tile(kt)
        sc_ref[tile_rows(kt), :] = s
        return (jnp.minimum(smin, jnp.min(s, axis=0, keepdims=True)),
                jnp.maximum(smax, jnp.max(s, axis=0, keepdims=True)))

    smin, smax = lax.fori_loop(
        0, qi, score_far,
        (jnp.full((1, t), jnp.inf, F32), jnp.full((1, t), -jnp.inf, F32)))
    s = score_tile(qi)
    causal = row <= colq
    sc_ref[tile_rows(qi), :] = jnp.where(causal, s, -jnp.inf)
    smin = jnp.minimum(smin, jnp.min(jnp.where(causal, s, jnp.inf), axis=0, keepdims=True))
    smax = jnp.maximum(smax, jnp.max(jnp.where(causal, s, -jnp.inf), axis=0, keepdims=True))

    def count_ge(thr):
        def body(kt, acc):
            m = sc_ref[tile_rows(kt), :] >= thr
            return acc + jnp.sum(jnp.where(m, 1.0, 0.0).reshape(t // 8, 8, t), axis=0)
        acc = lax.fori_loop(0, nk, body, jnp.zeros((8, t), F32))
        return jnp.sum(acc, axis=0, keepdims=True)

    def count_gt(thr):
        def body(kt, acc):
            m = sc_ref[tile_rows(kt), :] > thr
            return acc + jnp.sum(jnp.where(m, 1.0, 0.0).reshape(t // 8, 8, t), axis=0)
        acc = lax.fori_loop(0, nk, body, jnp.zeros((8, t), F32))
        return jnp.sum(acc, axis=0, keepdims=True)

    def min_where(pred_fn):
        def body(kt, acc):
            s = sc_ref[tile_rows(kt), :]
            return jnp.minimum(acc, jnp.min(jnp.where(pred_fn(s), s, jnp.inf), axis=0, keepdims=True))
        return lax.fori_loop(0, nk, body, jnp.full((1, t), jnp.inf, F32))

    def bisect(st):
        lo, hi, c_lo = st
        mid = lo + 0.5 * (hi - lo)
        cnt = count_ge(mid)
        up = cnt >= kq
        return jnp.where(up, mid, lo), jnp.where(up, hi, mid), jnp.where(up, cnt, c_lo)

    st = lax.fori_loop(0, BISECT_MIN_ITERS, lambda _, s_: bisect(s_),
                       (smin, smax, (pos + 1).astype(F32)))

    def more_cond(c):
        return (c[3] > 0.0) & (c[4] < BISECT_MAX_ITERS)

    def more_body(c):
        st_ = (c[0], c[1], c[2])
        flag = jnp.max(c[2] - kq)
        return bisect(st_) + (flag, c[4] + 1)

    lo, _, c_lo, _, _ = lax.while_loop(
        more_cond, more_body, st + (jnp.float32(1.0), jnp.int32(BISECT_MIN_ITERS)))
    surplus = jnp.max(c_lo - kq)

    @pl.when(surplus <= 0.0)
    def _():
        def body(kt, c):
            s = sc_ref[tile_rows(kt), :]
            sc_ref[tile_rows(kt), :] = jnp.where(s >= lo, 0.0, NEG)
            return c
        lax.fori_loop(0, nk, body, 0)

    @pl.when(surplus > 0.0)
    def _():
        def peel_cond(st):
            return st[3] > 0.0

        def peel(st):
            lo_, c_lo_, _, _ = st
            taup = min_where(lambda s: s >= lo_)
            n_gt = count_gt(taup)
            need_peel = (c_lo_ != kq) & (n_gt >= kq)
            nxt = min_where(lambda s: s > taup)
            lo_n = jnp.where(need_peel, nxt, lo_)
            c_n = jnp.where(need_peel, n_gt, c_lo_)
            n_gt_keep = jnp.where(need_peel, 0.0, n_gt)
            return lo_n, c_n, n_gt_keep, jnp.max(jnp.where(need_peel, 1.0, 0.0))

        lo2, c2, _, _ = lax.while_loop(
            peel_cond, peel, (lo, c_lo, jnp.zeros((1, t), F32), jnp.float32(1.0)))
        taup = min_where(lambda s: s >= lo2)
        n_gt = count_gt(taup)
        tie = c2 != kq
        need = kq - n_gt

        def count_tie_le(cut):
            def body(kt, acc):
                s = sc_ref[tile_rows(kt), :]
                kidx = (kt * t + row).astype(F32)
                m = (s == taup) & (kidx <= cut)
                return acc + jnp.sum(jnp.where(m, 1.0, 0.0).reshape(t // 8, 8, t), axis=0)
            acc = lax.fori_loop(0, nk, body, jnp.zeros((8, t), F32))
            return jnp.sum(acc, axis=0, keepdims=True)

        def ibisect(_, st):
            lo_i, hi_i = st
            mid_i = jnp.floor(0.5 * (lo_i + hi_i))
            ok = count_tie_le(mid_i) >= need
            return jnp.where(ok, lo_i, mid_i), jnp.where(ok, mid_i, hi_i)

        n_it = int(math.ceil(math.log2(seq))) + 1
        _, cut = lax.fori_loop(0, n_it, ibisect,
                               (jnp.full((1, t), -1.0, F32), jnp.full((1, t), float(seq - 1), F32)))
        cut = jnp.where(tie, cut, float(seq))

        def body(kt, c):
            s = sc_ref[tile_rows(kt), :]
            kidx = (kt * t + row).astype(F32)
            keep = (s >= lo2) & jnp.logical_not((s == taup) & (kidx > cut))
            sc_ref[tile_rows(kt), :] = jnp.where(keep, 0.0, NEG)
            return c
        lax.fori_loop(0, nk, body, 0)

    m_ref[...] = jnp.full_like(m_ref, NEG)
    l_ref[...] = jnp.zeros_like(l_ref)
    acc_ref[...] = jnp.zeros_like(acc_ref)

    def kv_tile(kt, near):
        rows = tile_rows(kt)
        m_new = []
        for h in range(n_heads):
            pair = slice((h // 2) * 2 * HEAD_DIM, (h // 2 + 1) * 2 * HEAD_DIM)
            s = jnp.dot(k_ref[0, rows, pair], qp_ref[h], preferred_element_type=F32)
            s = s + sc_ref[rows, :]
            if near is not None:
                s = s + tz_ref[near, h]
            s_ref[h] = s
            m_new.append(jnp.maximum(m_ref[h:h + 1, :], jnp.max(s, axis=0, keepdims=True)))
        for h in range(n_heads):
            hs = slice(h * HEAD_DIM, (h + 1) * HEAD_DIM)
            corr = jnp.exp2(m_ref[h:h + 1, :] - m_new[h])
            p = jnp.exp2(s_ref[h] - m_new[h])
            l_ref[h:h + 1, :] = l_ref[h:h + 1, :] * corr + jnp.sum(p, axis=0, keepdims=True)
            acc_ref[hs, :] = acc_ref[hs, :] * corr + jnp.dot(
                vt_ref[0, hs, rows], p.astype(BF16), preferred_element_type=F32)
            m_ref[h:h + 1, :] = m_new[h]

    n_far = jnp.maximum(qi - 1, 0)

    def far_body(kt, c):
        kv_tile(kt, None)
        return c

    def near_body(kt, c):
        kv_tile(kt, qi - kt)
        return c

    lax.fori_loop(0, n_far, far_body, 0)
    lax.fori_loop(n_far, nk, near_body, 0)
    rcp = 1.0 / l_ref[...]
    rcp = jnp.broadcast_to(rcp[:, None, :], (n_heads, HEAD_DIM, t)).reshape(n_heads * HEAD_DIM, t)
    out = acc_ref[...] * rcp * _silu(azt_ref[0].astype(F32))
    o_ref[0] = out.T.astype(BF16)


def _dsa_mixer(k, vt, kidx, qt, qit, azt, wt, tz, topk):
    bsz, seq, width = k.shape
    n_heads = width // HEAD_DIM
    t = ATT_TILE
    per_b = lambda shape: pl.BlockSpec((1,) + shape, lambda b, q: (b, 0, 0))
    qtile = lambda r: pl.BlockSpec((1, r, t), lambda b, q: (b, 0, q))
    kern = functools.partial(_dsa_kernel, topk=topk, n_heads=n_heads)
    return pl.pallas_call(
        kern,
        out_shape=jax.ShapeDtypeStruct((bsz, seq, width), BF16),
        grid=(bsz, seq // t),
        in_specs=[per_b((seq, width)), per_b((width, seq)), per_b((seq, LANES)),
                  qtile(width), qtile(width), qtile(width), qtile(IDX_HEADS),
                  pl.BlockSpec((2, n_heads, t, t), lambda b, q: (0, 0, 0, 0))],
        out_specs=pl.BlockSpec((1, t, width), lambda b, q: (b, q, 0)),
        scratch_shapes=[pltpu.VMEM((seq, t), F32),
                        pltpu.VMEM((IDX_HEADS, 2 * IDX_DIM, t), BF16),
                        pltpu.VMEM((n_heads, 2 * HEAD_DIM, t), BF16),
                        pltpu.VMEM((n_heads, t), F32), pltpu.VMEM((n_heads, t), F32),
                        pltpu.VMEM((width, t), F32), pltpu.VMEM((n_heads, t, t), F32)],
        compiler_params=_cparams(("parallel", "arbitrary")),
        name="dsa_mixer",
    )(k, vt, kidx, qt, qit, azt, wt, tz)


def _out_kernel(x_ref, ys_ref, ya_ref, w1_ref, w2_ref, mod_ref, o_ref, *, d_model):
    gate = mod_ref[0, :, 2 * d_model:3 * d_model]
    y = (jnp.dot(ys_ref[0], w1_ref[...], preferred_element_type=F32)
         + jnp.dot(ya_ref[0], w2_ref[...], preferred_element_type=F32))
    o_ref[0] = x_ref[0] + gate * y


def _out_proj(x, y_ssm, y_attn, w_out, mod):
    bsz, seq, d = x.shape
    width = y_ssm.shape[-1]
    tm = min(OUT_TILE, seq)
    tok = lambda w: pl.BlockSpec((1, tm, w), lambda b, s: (b, s, 0))
    const = lambda shape: pl.BlockSpec(shape, lambda b, s: (0,) * len(shape))
    return pl.pallas_call(
        functools.partial(_out_kernel, d_model=d),
        out_shape=jax.ShapeDtypeStruct((bsz, seq, d), x.dtype),
        grid=(bsz, seq // tm),
        in_specs=[tok(d), tok(width), tok(width), const((width, d)), const((width, d)),
                  pl.BlockSpec((1, 1, 3 * d), lambda b, s: (b, 0, 0))],
        out_specs=tok(d),
        compiler_params=_cparams(("parallel", "arbitrary")),
        name="out_proj",
    )(x, y_ssm, y_attn, w_out[:width].astype(BF16), w_out[width:].astype(BF16),
      mod.reshape(bsz, 1, 3 * d))


def kernel(x, c, rel_bias, norm_g, w_ada, b_ada, w_in, q_gain, k_gain, a_re, a_im, log_dt,
           b_re, b_im, c_re, c_im, d_skip, w_glu, b_glu, w_out):
    bsz, seq, d = x.shape
    depth = w_in.shape[0]
    width = d // 2
    n_heads = width // HEAD_DIM
    topk = min(TOPK_MAX, seq // 4)
    assert seq % ATT_TILE == 0 and seq % SSM_CHUNK == 0
    assert ATT_TILE + 1 >= MAX_DISTANCE
    tz = _bias_prep(rel_bias, n_heads)
    for l in range(depth):
        mod = _adaln_mod(c, w_ada[l], b_ada[l])
        u, zs, k, kidx, qt, vt, qit, azt, wt = _in_proj(x, mod, norm_g[l], w_in[l],
                                                        q_gain[l], k_gain[l])
        tabs = _ssm_prep(a_re[l], a_im[l], log_dt[l], b_re[l], b_im[l])
        y_ssm = _s5_mixer(u, zs, tabs, c_re[l], c_im[l], d_skip[l], w_glu[l], b_glu[l])
        y_attn = _dsa_mixer(k, vt, kidx, qt, qit, azt, wt, tz, topk)
        x = _out_proj(x, y_ssm, y_attn, w_out[l], mod)
    return x
```

```python
import functools
import math

import jax
import jax.numpy as jnp
import numpy as np
from jax import lax
from jax.experimental import pallas as pl
from jax.experimental.pallas import tpu as pltpu

F32 = jnp.float32
BF16 = jnp.bfloat16

SSM_GROUP = 16
SSM_STATE = 64
HEAD_DIM = 64
IDX_HEADS = 8
IDX_DIM = 64
TOPK_MAX = 256
NUM_BUCKETS = 32
MAX_DISTANCE = 128
EPS = 1e-6

LANES = 128
V7X_VMEM_LIMIT_BYTES = 56 * 1024 * 1024

SSM_CHUNK = 64
SSM_TILE = 256
SSM_BLOCK_CH = 128
PROJ_TILE = 512
ATT_TILE = 256
OUT_TILE = 512
BISECT_MIN_ITERS = 12
BISECT_MAX_ITERS = 26
NEG = -1e30
LOG2E = math.log2(math.e)

NT_DIMS = (((1,), (1,)), ((), ()))


def _cparams(sem):
    return pltpu.CompilerParams(dimension_semantics=sem,
                                vmem_limit_bytes=V7X_VMEM_LIMIT_BYTES)


def _sigmoid(x):
    return 1.0 / (1.0 + jnp.exp(-x))


def _silu(x):
    return x * _sigmoid(x)


def _ssm_prep_kernel(are_r, aim_r, ldt_r, are_c, aim_c, ldt_c, bre, bim,
                     ep_re, ep_im, em_re, em_im, laml_re, laml_im, bb_re, bb_im):
    dt = jnp.exp(ldt_r[...])
    ar = are_r[...] * dt
    ai = aim_r[...] * dt
    n = ar.shape[-1]
    tau = lax.broadcasted_iota(jnp.int32, (SSM_CHUNK, n), 0).astype(F32)
    mag_p = jnp.exp(tau * ar)
    mag_m = jnp.exp(-tau * ar)
    cs = jnp.cos(tau * ai)
    sn = jnp.sin(tau * ai)
    ep_re[...] = mag_p * cs
    ep_im[...] = mag_p * sn
    em_re[...] = mag_m * cs
    em_im[...] = -(mag_m * sn)
    mag_l = jnp.exp(float(SSM_CHUNK) * ar)
    laml_re[...] = mag_l * jnp.cos(float(SSM_CHUNK) * ai)
    laml_im[...] = mag_l * jnp.sin(float(SSM_CHUNK) * ai)
    a = are_c[...]
    b = aim_c[...]
    dtc = jnp.exp(ldt_c[...])
    mag = jnp.exp(a * dtc)
    x = mag * jnp.cos(b * dtc) - 1.0
    y = mag * jnp.sin(b * dtc)
    den = a * a + b * b
    cre = (x * a + y * b) / den
    cim = (y * a - x * b) / den
    bb_re[...] = cre * bre[...] - cim * bim[...]
    bb_im[...] = cre * bim[...] + cim * bre[...]


def _ssm_prep(a_re, a_im, log_dt, b_re, b_im):
    g, p = a_re.shape
    n = g * p
    ldt = jnp.broadcast_to(log_dt[:, None], (g, p))
    row = lambda t: t.reshape(1, n)
    col = lambda t: t.reshape(n, 1)
    tab = jax.ShapeDtypeStruct((SSM_CHUNK, n), F32)
    vec = jax.ShapeDtypeStruct((1, n), F32)
    bsh = jax.ShapeDtypeStruct((n, SSM_GROUP), F32)
    return pl.pallas_call(
        _ssm_prep_kernel,
        out_shape=(tab, tab, tab, tab, vec, vec, bsh, bsh),
        name="ssm_prep",
    )(row(a_re), row(a_im), row(ldt), col(a_re), col(a_im), col(ldt),
      b_re.reshape(n, SSM_GROUP), b_im.reshape(n, SSM_GROUP))


def _bias_prep_kernel(rb_ref, o_ref):
    t = o_ref.shape[-1]
    n_heads = o_ref.shape[1]
    i = lax.broadcasted_iota(jnp.int32, (t, t), 0)
    j = lax.broadcasted_iota(jnp.int32, (t, t), 1)
    max_exact = NUM_BUCKETS // 2

    def bias(idx, h):
        dist = jnp.maximum(idx * t + j - i, 0)
        d = jnp.maximum(dist, 1).astype(F32)
        large = max_exact + (jnp.log(d / max_exact) / math.log(MAX_DISTANCE / max_exact)
                             * (NUM_BUCKETS - max_exact)).astype(jnp.int32)
        large = jnp.minimum(large, NUM_BUCKETS - 1)
        bucket = jnp.where(dist < max_exact, dist, large)
        acc = jnp.zeros((t, t), F32)
        for b in range(NUM_BUCKETS):
            acc = jnp.where(bucket == b, rb_ref[b, h], acc)
        return acc

    for h in range(n_heads):
        far = bias(2, h)
        for idx in range(2):
            o_ref[idx, h] = (bias(idx, h) - far) * LOG2E


def _bias_prep(rel_bias, n_heads):
    t = ATT_TILE
    return pl.pallas_call(
        _bias_prep_kernel,
        out_shape=jax.ShapeDtypeStruct((2, n_heads, t, t), F32),
        in_specs=[pl.BlockSpec(memory_space=pltpu.SMEM)],
        name="bias_prep",
    )(rel_bias)


def _mod_kernel(c_ref, w_ref, b_ref, o_ref):
    c = c_ref[...]
    cond = _silu(c)
    o_ref[...] = jnp.dot(cond.astype(BF16), w_ref[...].astype(BF16),
                         preferred_element_type=F32) + b_ref[...]


def _adaln_mod(c, w_ada, b_ada):
    bsz, d = c.shape
    n = w_ada.shape[1]
    tn = 512
    return pl.pallas_call(
        _mod_kernel,
        out_shape=jax.ShapeDtypeStruct((bsz, n), F32),
        grid=(n // tn,),
        in_specs=[pl.BlockSpec((bsz, d), lambda i: (0, 0)),
                  pl.BlockSpec((d, tn), lambda i: (0, i)),
                  pl.BlockSpec((1, tn), lambda i: (0, i))],
        out_specs=pl.BlockSpec((bsz, tn), lambda i: (0, i)),
        compiler_params=_cparams(("arbitrary",)),
        name="adaln_mod",
    )(c, w_ada, b_ada.reshape(1, n))


def _in_proj_kernel(x_ref, mod_ref, g_ref, wnat_ref, wt_ref, ww_ref, mblk_ref,
                    kg_ref, qg_ref,
                    u_ref, zs_ref, k_ref, kidx_ref, qt_ref, vt_ref, qit_ref, azt_ref, wt_out_ref,
                    *, d_model, width, n_heads):
    x = x_ref[0]
    shift = mod_ref[0, :, 0:d_model]
    scale = mod_ref[0, :, d_model:2 * d_model]
    ms = jnp.mean(x * x, axis=-1, keepdims=True)
    xn = x * lax.rsqrt(ms + EPS) * g_ref[...]
    h = (xn * (1.0 + scale) + shift).astype(BF16)
    tm = h.shape[0]

    nat = jnp.dot(h, wnat_ref[...], preferred_element_type=F32)
    u_ref[0] = nat[:, 0:width].astype(BF16)
    zs_ref[0] = nat[:, width:2 * width].astype(BF16)
    k = nat[:, 2 * width:3 * width]
    kms = jnp.dot((k * k).astype(BF16), mblk_ref[...], preferred_element_type=F32)
    k_ref[0] = (k * lax.rsqrt(kms + EPS) * kg_ref[...]).astype(BF16)
    kidx_ref[0] = nat[:, 3 * width:3 * width + LANES].astype(BF16)

    tr = lax.dot_general(wt_ref[...], h, NT_DIMS, preferred_element_type=F32)
    q3 = tr[0:width].reshape(n_heads, HEAD_DIM, tm)
    qms = jnp.mean(q3 * q3, axis=1, keepdims=True)
    qn = q3 * lax.rsqrt(qms + EPS) * qg_ref[...][None] * (HEAD_DIM ** -0.5 * LOG2E)
    qt_ref[0] = qn.reshape(width, tm).astype(BF16)
    vt_ref[0] = tr[width:2 * width].astype(BF16)
    qit_ref[0] = tr[2 * width:3 * width].astype(BF16)
    azt_ref[0] = tr[3 * width:4 * width].astype(BF16)
    wt_out_ref[0] = lax.dot_general(ww_ref[...], h, NT_DIMS,
                                    preferred_element_type=F32) * (IDX_HEADS ** -0.5)


def _in_proj(x, mod, norm_g, w_in, q_gain, k_gain):
    bsz, seq, d = x.shape
    width = d // 2
    n_heads = width // HEAD_DIM
    tm = min(PROJ_TILE, seq)
    o = np.cumsum([0, width, width, width, width, width, width,
                   IDX_HEADS * IDX_DIM, IDX_DIM, IDX_HEADS])
    col = lambda i: w_in[:, int(o[i]):int(o[i + 1])]
    w_kidx = jnp.pad(col(7), ((0, 0), (0, LANES - IDX_DIM)))
    w_nat = jnp.concatenate([col(0), col(1), col(3), w_kidx], axis=1).astype(BF16)
    w_t = jnp.concatenate([col(2), col(4), col(6), col(5)], axis=1).T.astype(BF16)
    w_w = col(8).T.astype(BF16)
    hid = np.arange(width) // HEAD_DIM
    mblk = jnp.asarray((hid[:, None] == hid[None, :]).astype(np.float32) / HEAD_DIM, BF16)
    kg = jnp.tile(k_gain, n_heads).reshape(1, width)
    qg = q_gain.reshape(HEAD_DIM, 1)

    nat_w = w_nat.shape[1]
    bs = jax.ShapeDtypeStruct
    const = lambda shape: pl.BlockSpec(shape, lambda b, s: (0,) * len(shape))
    tok = lambda w: pl.BlockSpec((1, tm, w), lambda b, s: (b, s, 0))
    trn = lambda r: pl.BlockSpec((1, r, tm), lambda b, s: (b, 0, s))
    kern = functools.partial(_in_proj_kernel, d_model=d, width=width, n_heads=n_heads)
    return pl.pallas_call(
        kern,
        out_shape=(bs((bsz, seq, width), BF16), bs((bsz, seq, width), BF16),
                   bs((bsz, seq, width), BF16), bs((bsz, seq, LANES), BF16),
                   bs((bsz, width, seq), BF16), bs((bsz, width, seq), BF16),
                   bs((bsz, width, seq), BF16), bs((bsz, width, seq), BF16),
                   bs((bsz, IDX_HEADS, seq), F32)),
        grid=(bsz, seq // tm),
        in_specs=[tok(d),
                  pl.BlockSpec((1, 1, 3 * d), lambda b, s: (b, 0, 0)),
                  const((1, d)), const((d, nat_w)), const((4 * width, d)),
                  const((IDX_HEADS, d)), const((width, width)),
                  const((1, width)), const((HEAD_DIM, 1))],
        out_specs=(tok(width), tok(width), tok(width), tok(LANES),
                   trn(width), trn(width), trn(width), trn(width), trn(IDX_HEADS)),
        compiler_params=_cparams(("parallel", "arbitrary")),
        name="in_proj",
    )(x, mod.reshape(bsz, 1, 3 * d), norm_g.reshape(1, d), w_nat, w_t, w_w, mblk, kg, qg)


def _s5_kernel(u_ref, zs_ref, bblk_ref, cre_ref, cim_ref, ep_re, ep_im, em_re, em_im,
               laml_re, laml_im, tri_ref, dsk_ref, wglu_ref, bglu_ref,
               o_ref, car_re, car_im, *, n_blocks):
    tm = u_ref.shape[1]
    nch = tm // SSM_CHUNK
    nst = cre_ref.shape[1]

    @pl.when(pl.program_id(1) == 0)
    def _():
        car_re[...] = jnp.zeros_like(car_re)
        car_im[...] = jnp.zeros_like(car_im)

    u = u_ref[0]
    tri = tri_ref[...]
    ys = []
    for j in range(n_blocks):
        st = slice(j * nst, (j + 1) * nst)
        uj = u[:, j * SSM_BLOCK_CH:(j + 1) * SSM_BLOCK_CH]
        p = jnp.dot(uj, bblk_ref[j], preferred_element_type=F32)
        p_re = p[:, :nst].reshape(nch, SSM_CHUNK, nst)
        p_im = p[:, nst:].reshape(nch, SSM_CHUNK, nst)
        emr = em_re[:, st][None]
        emi = em_im[:, st][None]
        w_re = (p_re * emr - p_im * emi).reshape(tm, nst).astype(BF16)
        w_im = (p_re * emi + p_im * emr).reshape(tm, nst).astype(BF16)
        cum_re = jnp.dot(tri, w_re, preferred_element_type=F32).reshape(nch, SSM_CHUNK, nst)
        cum_im = jnp.dot(tri, w_im, preferred_element_type=F32).reshape(nch, SSM_CHUNK, nst)
        lr = laml_re[:, st]
        li = laml_im[:, st]
        cr = car_re[:, st]
        ci = car_im[:, st]
        crs, cis = [], []
        for c in range(nch):
            crs.append(cr)
            cis.append(ci)
            tr_ = cum_re[c, SSM_CHUNK - 1:SSM_CHUNK, :] + cr
            ti_ = cum_im[c, SSM_CHUNK - 1:SSM_CHUNK, :] + ci
            cr = lr * tr_ - li * ti_
            ci = lr * ti_ + li * tr_
        car_re[:, st] = cr
        car_im[:, st] = ci
        a_re = cum_re + jnp.stack(crs, axis=0)
        a_im = cum_im + jnp.stack(cis, axis=0)
        epr = ep_re[:, st][None]
        epi = ep_im[:, st][None]
        x_re = (a_re * epr - a_im * epi).reshape(tm, nst).astype(BF16)
        x_im = (a_re * epi + a_im * epr).reshape(tm, nst).astype(BF16)
        ys.append(jnp.dot(x_re, cre_ref[j], preferred_element_type=F32)
                  - jnp.dot(x_im, cim_ref[j], preferred_element_type=F32))
    y = jnp.concatenate(ys, axis=1) + dsk_ref[...] * u.astype(F32)
    z = 0.5 * y * (1.0 + jnp.tanh(math.sqrt(2.0 / math.pi) * (y + 0.044715 * (y * y * y))))
    gl = jnp.dot(z.astype(BF16), wglu_ref[...], preferred_element_type=F32) + bglu_ref[...]
    zz = z * _sigmoid(gl)
    o_ref[0] = (zz * _silu(zs_ref[0].astype(F32))).astype(BF16)


def _s5_mixer(u, zs, tabs, c_re, c_im, d_skip, w_glu, b_glu):
    bsz, seq, width = u.shape
    ep_re, ep_im, em_re, em_im, laml_re, laml_im, bb_re, bb_im = tabs
    tm = min(SSM_TILE, seq)
    gpb = SSM_BLOCK_CH // SSM_GROUP
    n_blocks = width // SSM_BLOCK_CH
    nst = gpb * SSM_STATE
    eye = jnp.eye(gpb, dtype=F32)

    def b_block(t):
        t4 = t.reshape(n_blocks, gpb, SSM_STATE, SSM_GROUP)
        return jnp.einsum('jgpc,gh->jgchp', t4, eye).reshape(n_blocks, SSM_BLOCK_CH, nst)

    def c_block(t):
        t4 = t.reshape(n_blocks, gpb, SSM_GROUP, SSM_STATE)
        return jnp.einsum('jgcp,gh->jhpgc', t4, eye).reshape(n_blocks, nst, SSM_BLOCK_CH)

    bblk = jnp.concatenate([b_block(bb_re), b_block(bb_im)], axis=2).astype(BF16)
    cre = c_block(c_re).astype(BF16)
    cim = c_block(c_im).astype(BF16)
    t_idx = np.arange(tm)
    tri = jnp.asarray(((t_idx[:, None] >= t_idx[None, :])
                       & (t_idx[:, None] // SSM_CHUNK == t_idx[None, :] // SSM_CHUNK)
                       ).astype(np.float32), BF16)
    n = ep_re.shape[1]
    const = lambda shape: pl.BlockSpec(shape, lambda b, s: (0,) * len(shape))
    tok = pl.BlockSpec((1, tm, width), lambda b, s: (b, s, 0))
    kern = functools.partial(_s5_kernel, n_blocks=n_blocks)
    return pl.pallas_call(
        kern,
        out_shape=jax.ShapeDtypeStruct((bsz, seq, width), BF16),
        grid=(bsz, seq // tm),
        in_specs=[tok, tok,
                  const((n_blocks, SSM_BLOCK_CH, 2 * nst)),
                  const((n_blocks, nst, SSM_BLOCK_CH)), const((n_blocks, nst, SSM_BLOCK_CH)),
                  const((SSM_CHUNK, n)), const((SSM_CHUNK, n)),
                  const((SSM_CHUNK, n)), const((SSM_CHUNK, n)),
                  const((1, n)), const((1, n)), const((tm, tm)),
                  const((1, width)), const((width, width)), const((1, width))],
        out_specs=tok,
        scratch_shapes=[pltpu.VMEM((1, n), F32), pltpu.VMEM((1, n), F32)],
        compiler_params=_cparams(("parallel", "arbitrary")),
        name="s5_mixer",
    )(u, zs, bblk, cre, cim, ep_re, ep_im, em_re, em_im, laml_re, laml_im, tri,
      d_skip.reshape(1, width), w_glu.astype(BF16), b_glu.reshape(1, width))


def _dsa_kernel(k_ref, vt_ref, kidx_ref, qt_ref, qit_ref, azt_ref, wt_ref, tz_ref,
                o_ref, sc_ref, qip_ref, qp_ref, m_ref, l_ref, acc_ref, s_ref, sel_ref,
                *, topk, n_heads):
    t = ATT_TILE
    qi = pl.program_id(1)
    nk = qi + 1
    seq = k_ref.shape[1]
    row = lax.broadcasted_iota(jnp.int32, (t, t), 0)
    colq = lax.broadcasted_iota(jnp.int32, (t, t), 1)
    lane = lax.broadcasted_iota(jnp.int32, (1, t), 1)
    pos = qi * t + lane
    kq = jnp.minimum(pos + 1, topk).astype(F32)

    def tile_rows(kt):
        return pl.ds(pl.multiple_of(kt * t, t), t)

    zpad = jnp.zeros((HEAD_DIM, t), BF16)
    for h in range(IDX_HEADS):
        qip_ref[h] = jnp.concatenate([qit_ref[0, h * IDX_DIM:(h + 1) * IDX_DIM, :], zpad], axis=0)
    for h in range(n_heads):
        qh = qt_ref[0, h * HEAD_DIM:(h + 1) * HEAD_DIM, :]
        qp_ref[h] = jnp.concatenate([qh, zpad] if h % 2 == 0 else [zpad, qh], axis=0)

    def score_tile(kt):
        kx = kidx_ref[0, tile_rows(kt), :]
        s = jnp.zeros((t, t), F32)
        for h in range(IDX_HEADS):
            rel = jnp.dot(kx, qip_ref[h], preferred_element_type=F32)
            s = s + wt_ref[0, h:h + 1, :] * jnp.maximum(rel, 0.0)
        return s

    def score_far(kt, carry):
        smin, smax = carry
        s = score_tile(kt)
        sc_ref[tile_rows(kt), :] = s
        return (jnp.minimum(smin, jnp.min(s, axis=0, keepdims=True)),
                jnp.maximum(smax, jnp.max(s, axis=0, keepdims=True)))

    smin, smax = lax.fori_loop(
        0, qi, score_far,
        (jnp.full((1, t), jnp.inf, F32), jnp.full((1, t), -jnp.inf, F32)))
    s = score_tile(qi)
    causal = row <= colq
    sc_ref[tile_rows(qi), :] = jnp.where(causal, s, -jnp.inf)
    smin = jnp.minimum(smin, jnp.min(jnp.where(causal, s, jnp.inf), axis=0, keepdims=True))
    smax = jnp.maximum(smax, jnp.max(jnp.where(causal, s, -jnp.inf), axis=0, keepdims=True))

    def count_where(pred_fn):
        def body(kt, acc):
            m = pred_fn(sc_ref[tile_rows(kt), :])
            return acc + jnp.sum(jnp.where(m, 1.0, 0.0).reshape(4, t // 32, 8, t), axis=1)
        acc = lax.fori_loop(0, nk, body, jnp.zeros((4, 8, t), F32))
        return jnp.sum(acc.reshape(32, t), axis=0, keepdims=True)

    def count_ge(thr):
        return count_where(lambda s_: s_ >= thr)

    def count_gt(thr):
        return count_where(lambda s_: s_ > thr)

    def min_where(pred_fn):
        def body(kt, acc):
            s = sc_ref[tile_rows(kt), :]
            return jnp.minimum(acc, jnp.min(jnp.where(pred_fn(s), s, jnp.inf), axis=0, keepdims=True))
        return lax.fori_loop(0, nk, body, jnp.full((1, t), jnp.inf, F32))

    def bisect(st):
        lo, hi, c_lo = st
        mid = lo + 0.5 * (hi - lo)
        cnt = count_ge(mid)
        up = cnt >= kq
        return jnp.where(up, mid, lo), jnp.where(up, hi, mid), jnp.where(up, cnt, c_lo)

    c_ge0 = count_ge(0.0)
    c_gt0 = count_gt(0.0)
    above = c_gt0 >= kq
    below = c_ge0 < kq
    at_zero = jnp.logical_not(above | below)
    zero_tie = at_zero & (c_ge0 > kq)
    lo_neg = above & (smin <= 0.0)
    lo0 = jnp.where(at_zero | lo_neg, 0.0, smin)
    c_lo0 = jnp.where(at_zero | lo_neg, c_ge0, (pos + 1).astype(F32))
    hi0 = jnp.where(at_zero, 0.0, jnp.where(below, jnp.minimum(smax, 0.0), smax))
    open_f = jnp.where(zero_tie, 0.0, 1.0)

    st = lax.fori_loop(0, BISECT_MIN_ITERS, lambda _, s_: bisect(s_), (lo0, hi0, c_lo0))

    def more_cond(c):
        return (c[3] > 0.0) & (c[4] < BISECT_MAX_ITERS)

    def more_body(c):
        st_ = (c[0], c[1], c[2])
        flag = jnp.max((c[2] - kq) * open_f)
        return bisect(st_) + (flag, c[4] + 1)

    lo, _, c_lo, _, _ = lax.while_loop(
        more_cond, more_body, st + (jnp.float32(1.0), jnp.int32(BISECT_MIN_ITERS)))

    sel_ref[0:1, :] = lo
    sel_ref[1:2, :] = jnp.zeros((1, t), F32)
    sel_ref[2:3, :] = kq - c_gt0
    sel_ref[3:4, :] = jnp.where(zero_tie, 1.0, 0.0)

    @pl.when(jnp.max((c_lo - kq) * open_f) > 0.0)
    def _():
        def peel_cond(c):
            return c[6] > 0.0

        def peel(c):
            lo_, c_lo_, taup_, ngt_, gtie_ = c[0], c[1], c[2], c[3], c[4]
            taup_n = min_where(lambda s_: s_ >= lo_)
            ngt_n = count_gt(taup_n)
            active = (c_lo_ > kq) & (open_f > 0.0) & (gtie_ == 0.0)
            is_tie = active & (ngt_n < kq)
            need_peel = active & (ngt_n >= kq)
            nxt = min_where(lambda s_: s_ > taup_n)
            return (jnp.where(need_peel, nxt, lo_), jnp.where(need_peel, ngt_n, c_lo_),
                    jnp.where(is_tie, taup_n, taup_), jnp.where(is_tie, ngt_n, ngt_),
                    jnp.where(is_tie, 1.0, gtie_), c[5],
                    jnp.max(jnp.where(need_peel, 1.0, 0.0)))

        zero = jnp.zeros((1, t), F32)
        lo2, _, taup2, ngt2, gtie2, _, _ = lax.while_loop(
            peel_cond, peel, (lo, c_lo, zero, zero, zero, zero, jnp.float32(1.0)))
        gt = gtie2 > 0.0
        sel_ref[0:1, :] = lo2
        sel_ref[1:2, :] = jnp.where(gt, taup2, sel_ref[1:2, :])
        sel_ref[2:3, :] = jnp.where(gt, kq - ngt2, sel_ref[2:3, :])
        sel_ref[3:4, :] = jnp.where(gt, 1.0, sel_ref[3:4, :])

    lo_f = sel_ref[0:1, :]
    tie_f = sel_ref[3:4, :]
    any_tie = jnp.max(tie_f)

    @pl.when(any_tie <= 0.0)
    def _():
        def body(kt, c):
            s_ = sc_ref[tile_rows(kt), :]
            sc_ref[tile_rows(kt), :] = jnp.where(s_ >= lo_f, 0.0, NEG)
            return c
        lax.fori_loop(0, nk, body, 0)

    @pl.when(any_tie > 0.0)
    def _():
        taup = sel_ref[1:2, :]
        need = jnp.where(tie_f > 0.0, sel_ref[2:3, :], float(2 * seq))
        tri = jnp.where(colq <= row, 1.0, 0.0).astype(BF16)

        def body(kt, seen):
            s_ = sc_ref[tile_rows(kt), :]
            z = s_ == taup
            pref = jnp.dot(tri, jnp.where(z, 1.0, 0.0).astype(BF16), preferred_element_type=F32)
            keep = (s_ >= lo_f) & jnp.logical_not(z & (seen + pref > need))
            sc_ref[tile_rows(kt), :] = jnp.where(keep, 0.0, NEG)
            return seen + pref[t - 1:t, :]
        lax.fori_loop(0, nk, body, jnp.zeros((1, t), F32))

    m_ref[...] = jnp.full_like(m_ref, NEG)
    l_ref[...] = jnp.zeros_like(l_ref)
    acc_ref[...] = jnp.zeros_like(acc_ref)

    def kv_tile(kt, near):
        rows = tile_rows(kt)
        m_new = []
        for h in range(n_heads):
            pair = slice((h // 2) * 2 * HEAD_DIM, (h // 2 + 1) * 2 * HEAD_DIM)
            s = jnp.dot(k_ref[0, rows, pair], qp_ref[h], preferred_element_type=F32)
            s = s + sc_ref[rows, :]
            if near is not None:
                s = s + tz_ref[near, h]
            s_ref[h] = s
            m_new.append(jnp.maximum(m_ref[h:h + 1, :], jnp.max(s, axis=0, keepdims=True)))
        for h in range(n_heads):
            hs = slice(h * HEAD_DIM, (h + 1) * HEAD_DIM)
            corr = jnp.exp2(m_ref[h:h + 1, :] - m_new[h])
            p = jnp.exp2(s_ref[h] - m_new[h])
            l_ref[h:h + 1, :] = l_ref[h:h + 1, :] * corr + jnp.sum(p, axis=0, keepdims=True)
            acc_ref[hs, :] = acc_ref[hs, :] * corr + jnp.dot(
                vt_ref[0, hs, rows], p.astype(BF16), preferred_element_type=F32)
            m_ref[h:h + 1, :] = m_new[h]

    n_far = jnp.maximum(qi - 1, 0)

    def far_body(kt, c):
        kv_tile(kt, None)
        return c

    def near_body(kt, c):
        kv_tile(kt, qi - kt)
        return c

    lax.fori_loop(0, n_far, far_body, 0)
    lax.fori_loop(n_far, nk, near_body, 0)
    rcp = 1.0 / l_ref[...]
    rcp = jnp.broadcast_to(rcp[:, None, :], (n_heads, HEAD_DIM, t)).reshape(n_heads * HEAD_DIM, t)
    out = acc_ref[...] * rcp * _silu(azt_ref[0].astype(F32))
    o_ref[0] = out.T.astype(BF16)


def _dsa_mixer(k, vt, kidx, qt, qit, azt, wt, tz, topk):
    bsz, seq, width = k.shape
    n_heads = width // HEAD_DIM
    t = ATT_TILE
    per_b = lambda shape: pl.BlockSpec((1,) + shape, lambda b, q: (b, 0, 0))
    qtile = lambda r: pl.BlockSpec((1, r, t), lambda b, q: (b, 0, q))
    kern = functools.partial(_dsa_kernel, topk=topk, n_heads=n_heads)
    return pl.pallas_call(
        kern,
        out_shape=jax.ShapeDtypeStruct((bsz, seq, width), BF16),
        grid=(bsz, seq // t),
        in_specs=[per_b((seq, width)), per_b((width, seq)), per_b((seq, LANES)),
                  qtile(width), qtile(width), qtile(width), qtile(IDX_HEADS),
                  pl.BlockSpec((2, n_heads, t, t), lambda b, q: (0, 0, 0, 0))],
        out_specs=pl.BlockSpec((1, t, width), lambda b, q: (b, q, 0)),
        scratch_shapes=[pltpu.VMEM((seq, t), F32),
                        pltpu.VMEM((IDX_HEADS, 2 * IDX_DIM, t), BF16),
                        pltpu.VMEM((n_heads, 2 * HEAD_DIM, t), BF16),
                        pltpu.VMEM((n_heads, t), F32), pltpu.VMEM((n_heads, t), F32),
                        pltpu.VMEM((width, t), F32), pltpu.VMEM((n_heads, t, t), F32),
                        pltpu.VMEM((8, t), F32)],
        compiler_params=_cparams(("parallel", "arbitrary")),
        name="dsa_mixer",
    )(k, vt, kidx, qt, qit, azt, wt, tz)


def _out_kernel(x_ref, ys_ref, ya_ref, w1_ref, w2_ref, mod_ref, o_ref, *, d_model):
    gate = mod_ref[0, :, 2 * d_model:3 * d_model]
    y = (jnp.dot(ys_ref[0], w1_ref[...], preferred_element_type=F32)
         + jnp.dot(ya_ref[0], w2_ref[...], preferred_element_type=F32))
    o_ref[0] = x_ref[0] + gate * y


def _out_proj(x, y_ssm, y_attn, w_out, mod):
    bsz, seq, d = x.shape
    width = y_ssm.shape[-1]
    tm = min(OUT_TILE, seq)
    tok = lambda w: pl.BlockSpec((1, tm, w), lambda b, s: (b, s, 0))
    const = lambda shape: pl.BlockSpec(shape, lambda b, s: (0,) * len(shape))
    return pl.pallas_call(
        functools.partial(_out_kernel, d_model=d),
        out_shape=jax.ShapeDtypeStruct((bsz, seq, d), x.dtype),
        grid=(bsz, seq // tm),
        in_specs=[tok(d), tok(width), tok(width), const((width, d)), const((width, d)),
                  pl.BlockSpec((1, 1, 3 * d), lambda b, s: (b, 0, 0))],
        out_specs=tok(d),
        compiler_params=_cparams(("parallel", "arbitrary")),
        name="out_proj",
    )(x, y_ssm, y_attn, w_out[:width].astype(BF16), w_out[width:].astype(BF16),
      mod.reshape(bsz, 1, 3 * d))


def kernel(x, c, rel_bias, norm_g, w_ada, b_ada, w_in, q_gain, k_gain, a_re, a_im, log_dt,
           b_re, b_im, c_re, c_im, d_skip, w_glu, b_glu, w_out):
    bsz, seq, d = x.shape
    depth = w_in.shape[0]
    width = d // 2
    n_heads = width // HEAD_DIM
    topk = min(TOPK_MAX, seq // 4)
    assert seq % ATT_TILE == 0 and seq % SSM_CHUNK == 0
    assert ATT_TILE + 1 >= MAX_DISTANCE
    tz = _bias_prep(rel_bias, n_heads)
    for l in range(depth):
        mod = _adaln_mod(c, w_ada[l], b_ada[l])
        u, zs, k, kidx, qt, vt, qit, azt, wt = _in_proj(x, mod, norm_g[l], w_in[l],
                                                        q_gain[l], k_gain[l])
        tabs = _ssm_prep(a_re[l], a_im[l], log_dt[l], b_re[l], b_im[l])
        y_ssm = _s5_mixer(u, zs, tabs, c_re[l], c_im[l], d_skip[l], w_glu[l], b_glu[l])
        y_attn = _dsa_mixer(k, vt, kidx, qt, qit, azt, wt, tz, topk)
        x = _out_proj(x, y_ssm, y_attn, w_out[l], mod)
    return x
```

```python
import functools
import math

import jax
import jax.numpy as jnp
import numpy as np
from jax import lax
from jax.experimental import pallas as pl
from jax.experimental.pallas import tpu as pltpu

F32 = jnp.float32
BF16 = jnp.bfloat16

SSM_GROUP = 16
SSM_STATE = 64
HEAD_DIM = 64
IDX_HEADS = 8
IDX_DIM = 64
TOPK_MAX = 256
NUM_BUCKETS = 32
MAX_DISTANCE = 128
EPS = 1e-6

LANES = 128
V7X_VMEM_LIMIT_BYTES = 56 * 1024 * 1024

SSM_CHUNK = 128
SSM_TILE = 256
SSM_BLOCK_CH = 128
PROJ_TILE = 512
ATT_TILE = 256
OUT_TILE = 512
BISECT_MIN_ITERS = 12
BISECT_MAX_ITERS = 26
NEG = -1e30
LOG2E = math.log2(math.e)

NT_DIMS = (((1,), (1,)), ((), ()))


def _cparams(sem):
    return pltpu.CompilerParams(dimension_semantics=sem,
                                vmem_limit_bytes=V7X_VMEM_LIMIT_BYTES)


def _sigmoid(x):
    return 1.0 / (1.0 + jnp.exp(-x))


def _silu(x):
    return x * _sigmoid(x)


def _ssm_prep_kernel(are_r, aim_r, ldt_r, are_c, aim_c, ldt_c, bre, bim,
                     ep_re, ep_im, em_re, em_im, laml_re, laml_im, bb_re, bb_im):
    dt = jnp.exp(ldt_r[...])
    ar = are_r[...] * dt
    ai = aim_r[...] * dt
    n = ar.shape[-1]
    tau = lax.broadcasted_iota(jnp.int32, (SSM_CHUNK, n), 0).astype(F32)
    mag_p = jnp.exp(tau * ar)
    mag_m = jnp.exp(-tau * ar)
    cs = jnp.cos(tau * ai)
    sn = jnp.sin(tau * ai)
    ep_re[...] = mag_p * cs
    ep_im[...] = mag_p * sn
    em_re[...] = mag_m * cs
    em_im[...] = -(mag_m * sn)
    mag_l = jnp.exp(float(SSM_CHUNK) * ar)
    laml_re[...] = mag_l * jnp.cos(float(SSM_CHUNK) * ai)
    laml_im[...] = mag_l * jnp.sin(float(SSM_CHUNK) * ai)
    a = are_c[...]
    b = aim_c[...]
    dtc = jnp.exp(ldt_c[...])
    mag = jnp.exp(a * dtc)
    x = mag * jnp.cos(b * dtc) - 1.0
    y = mag * jnp.sin(b * dtc)
    den = a * a + b * b
    cre = (x * a + y * b) / den
    cim = (y * a - x * b) / den
    bb_re[...] = cre * bre[...] - cim * bim[...]
    bb_im[...] = cre * bim[...] + cim * bre[...]


def _ssm_prep(a_re, a_im, log_dt, b_re, b_im):
    g, p = a_re.shape
    n = g * p
    ldt = jnp.broadcast_to(log_dt[:, None], (g, p))
    row = lambda t: t.reshape(1, n)
    col = lambda t: t.reshape(n, 1)
    tab = jax.ShapeDtypeStruct((SSM_CHUNK, n), F32)
    vec = jax.ShapeDtypeStruct((1, n), F32)
    bsh = jax.ShapeDtypeStruct((n, SSM_GROUP), F32)
    return pl.pallas_call(
        _ssm_prep_kernel,
        out_shape=(tab, tab, tab, tab, vec, vec, bsh, bsh),
        name="ssm_prep",
    )(row(a_re), row(a_im), row(ldt), col(a_re), col(a_im), col(ldt),
      b_re.reshape(n, SSM_GROUP), b_im.reshape(n, SSM_GROUP))


def _bias_prep_kernel(rb_ref, o_ref):
    t = o_ref.shape[-1]
    n_heads = o_ref.shape[1]
    i = lax.broadcasted_iota(jnp.int32, (t, t), 0)
    j = lax.broadcasted_iota(jnp.int32, (t, t), 1)
    max_exact = NUM_BUCKETS // 2

    def bias(idx, h):
        dist = jnp.maximum(idx * t + j - i, 0)
        d = jnp.maximum(dist, 1).astype(F32)
        large = max_exact + (jnp.log(d / max_exact) / math.log(MAX_DISTANCE / max_exact)
                             * (NUM_BUCKETS - max_exact)).astype(jnp.int32)
        large = jnp.minimum(large, NUM_BUCKETS - 1)
        bucket = jnp.where(dist < max_exact, dist, large)
        acc = jnp.zeros((t, t), F32)
        for b in range(NUM_BUCKETS):
            acc = jnp.where(bucket == b, rb_ref[b, h], acc)
        return acc

    for h in range(n_heads):
        far = bias(2, h)
        for idx in range(3):
            o_ref[idx, h] = (bias(idx, h) - far) * LOG2E


def _bias_prep(rel_bias, n_heads):
    t = ATT_TILE
    return pl.pallas_call(
        _bias_prep_kernel,
        out_shape=jax.ShapeDtypeStruct((3, n_heads, t, t), F32),
        in_specs=[pl.BlockSpec(memory_space=pltpu.SMEM)],
        name="bias_prep",
    )(rel_bias)


def _mod_kernel(c_ref, w_ref, b_ref, o_ref):
    c = c_ref[...]
    cond = _silu(c)
    o_ref[...] = jnp.dot(cond.astype(BF16), w_ref[...].astype(BF16),
                         preferred_element_type=F32) + b_ref[...]


def _adaln_mod(c, w_ada, b_ada):
    bsz, d = c.shape
    n = w_ada.shape[1]
    tn = 512
    return pl.pallas_call(
        _mod_kernel,
        out_shape=jax.ShapeDtypeStruct((bsz, n), F32),
        grid=(n // tn,),
        in_specs=[pl.BlockSpec((bsz, d), lambda i: (0, 0)),
                  pl.BlockSpec((d, tn), lambda i: (0, i)),
                  pl.BlockSpec((1, tn), lambda i: (0, i))],
        out_specs=pl.BlockSpec((bsz, tn), lambda i: (0, i)),
        compiler_params=_cparams(("arbitrary",)),
        name="adaln_mod",
    )(c, w_ada, b_ada.reshape(1, n))


def _in_proj_kernel(x_ref, mod_ref, g_ref, wnat_ref, wt_ref, ww_ref, mblk_ref,
                    kg_ref, qg_ref,
                    u_ref, zs_ref, k_ref, kidx_ref, qt_ref, vt_ref, qit_ref, azt_ref, wt_out_ref,
                    *, d_model, width, n_heads):
    x = x_ref[0]
    shift = mod_ref[0, :, 0:d_model]
    scale = mod_ref[0, :, d_model:2 * d_model]
    ms = jnp.mean(x * x, axis=-1, keepdims=True)
    xn = x * lax.rsqrt(ms + EPS) * g_ref[...]
    h = (xn * (1.0 + scale) + shift).astype(BF16)
    tm = h.shape[0]

    nat = jnp.dot(h, wnat_ref[...], preferred_element_type=F32)
    u_ref[0] = nat[:, 0:width].astype(BF16)
    zs_ref[0] = nat[:, width:2 * width].astype(BF16)
    k = nat[:, 2 * width:3 * width]
    kms = jnp.dot((k * k).astype(BF16), mblk_ref[...], preferred_element_type=F32)
    k_ref[0] = (k * lax.rsqrt(kms + EPS) * kg_ref[...]).astype(BF16)
    kidx_ref[0] = nat[:, 3 * width:3 * width + LANES].astype(BF16)

    tr = lax.dot_general(wt_ref[...], h, NT_DIMS, preferred_element_type=F32)
    q3 = tr[0:width].reshape(n_heads, HEAD_DIM, tm)
    qms = jnp.mean(q3 * q3, axis=1, keepdims=True)
    qn = q3 * lax.rsqrt(qms + EPS) * qg_ref[...][None] * (HEAD_DIM ** -0.5 * LOG2E)
    qt_ref[0] = qn.reshape(width, tm).astype(BF16)
    vt_ref[0] = tr[width:2 * width].astype(BF16)
    qit_ref[0] = tr[2 * width:3 * width].astype(BF16)
    azt_ref[0] = tr[3 * width:4 * width].astype(BF16)
    wt_out_ref[0] = lax.dot_general(ww_ref[...], h, NT_DIMS,
                                    preferred_element_type=F32) * (IDX_HEADS ** -0.5)


def _in_proj(x, mod, norm_g, w_in, q_gain, k_gain):
    bsz, seq, d = x.shape
    width = d // 2
    n_heads = width // HEAD_DIM
    tm = min(PROJ_TILE, seq)
    o = np.cumsum([0, width, width, width, width, width, width,
                   IDX_HEADS * IDX_DIM, IDX_DIM, IDX_HEADS])
    col = lambda i: w_in[:, int(o[i]):int(o[i + 1])]
    w_kidx = jnp.pad(col(7), ((0, 0), (0, LANES - IDX_DIM)))
    w_nat = jnp.concatenate([col(0), col(1), col(3), w_kidx], axis=1).astype(BF16)
    w_t = jnp.concatenate([col(2), col(4), col(6), col(5)], axis=1).T.astype(BF16)
    w_w = col(8).T.astype(BF16)
    hid = np.arange(width) // HEAD_DIM
    mblk = jnp.asarray((hid[:, None] == hid[None, :]).astype(np.float32) / HEAD_DIM, BF16)
    kg = jnp.tile(k_gain, n_heads).reshape(1, width)
    qg = q_gain.reshape(HEAD_DIM, 1)

    nat_w = w_nat.shape[1]
    bs = jax.ShapeDtypeStruct
    const = lambda shape: pl.BlockSpec(shape, lambda b, s: (0,) * len(shape))
    tok = lambda w: pl.BlockSpec((1, tm, w), lambda b, s: (b, s, 0))
    trn = lambda r: pl.BlockSpec((1, r, tm), lambda b, s: (b, 0, s))
    kern = functools.partial(_in_proj_kernel, d_model=d, width=width, n_heads=n_heads)
    return pl.pallas_call(
        kern,
        out_shape=(bs((bsz, seq, width), BF16), bs((bsz, seq, width), BF16),
                   bs((bsz, seq, width), BF16), bs((bsz, seq, LANES), BF16),
                   bs((bsz, width, seq), BF16), bs((bsz, width, seq), BF16),
                   bs((bsz, width, seq), BF16), bs((bsz, width, seq), BF16),
                   bs((bsz, IDX_HEADS, seq), F32)),
        grid=(bsz, seq // tm),
        in_specs=[tok(d),
                  pl.BlockSpec((1, 1, 3 * d), lambda b, s: (b, 0, 0)),
                  const((1, d)), const((d, nat_w)), const((4 * width, d)),
                  const((IDX_HEADS, d)), const((width, width)),
                  const((1, width)), const((HEAD_DIM, 1))],
        out_specs=(tok(width), tok(width), tok(width), tok(LANES),
                   trn(width), trn(width), trn(width), trn(width), trn(IDX_HEADS)),
        compiler_params=_cparams(("parallel", "arbitrary")),
        name="in_proj",
    )(x, mod.reshape(bsz, 1, 3 * d), norm_g.reshape(1, d), w_nat, w_t, w_w, mblk, kg, qg)


def _s5_kernel(u_ref, zs_ref, bblk_ref, cre_ref, cim_ref, ep_re, ep_im, em_re, em_im,
               laml_re, laml_im, tri_ref, dsk_ref, wglu_ref, bglu_ref,
               o_ref, car_re, car_im, w_ref, x_ref, *, n_blocks):
    tm = u_ref.shape[1]
    nch = tm // SSM_CHUNK
    nst = cre_ref.shape[1]

    @pl.when(pl.program_id(1) == 0)
    def _():
        car_re[...] = jnp.zeros_like(car_re)
        car_im[...] = jnp.zeros_like(car_im)

    tri = tri_ref[...]
    for j in range(n_blocks):
        st = slice(j * nst, (j + 1) * nst)
        ch = slice(j * SSM_BLOCK_CH, (j + 1) * SSM_BLOCK_CH)
        p = jnp.dot(u_ref[0, :, ch], bblk_ref[j], preferred_element_type=F32)
        p_re = p[:, :nst].astype(BF16).reshape(nch, SSM_CHUNK, nst)
        p_im = p[:, nst:].astype(BF16).reshape(nch, SSM_CHUNK, nst)
        emr = em_re[:, st][None]
        emi = em_im[:, st][None]
        w_ref[j, :, 0:nst] = (p_re * emr - p_im * emi).reshape(tm, nst)
        w_ref[j, :, nst:2 * nst] = (p_re * emi + p_im * emr).reshape(tm, nst)
    for j in range(n_blocks):
        st = slice(j * nst, (j + 1) * nst)
        cum = jnp.dot(tri, w_ref[j], preferred_element_type=F32)
        cum_re = cum[:, :nst].reshape(nch, SSM_CHUNK, nst)
        cum_im = cum[:, nst:].reshape(nch, SSM_CHUNK, nst)
        lr = laml_re[:, st]
        li = laml_im[:, st]
        cr = car_re[:, st]
        ci = car_im[:, st]
        crs, cis = [], []
        for c in range(nch):
            crs.append(cr)
            cis.append(ci)
            tr_ = cum_re[c, SSM_CHUNK - 1:SSM_CHUNK, :] + cr
            ti_ = cum_im[c, SSM_CHUNK - 1:SSM_CHUNK, :] + ci
            cr = lr * tr_ - li * ti_
            ci = lr * ti_ + li * tr_
        car_re[:, st] = cr
        car_im[:, st] = ci
        a_re = (cum_re + jnp.stack(crs, axis=0)).astype(BF16)
        a_im = (cum_im + jnp.stack(cis, axis=0)).astype(BF16)
        epr = ep_re[:, st][None]
        epi = ep_im[:, st][None]
        x_ref[j, :, 0:nst] = (a_re * epr - a_im * epi).reshape(tm, nst)
        x_ref[j, :, nst:2 * nst] = (a_re * epi + a_im * epr).reshape(tm, nst)
    ys = []
    for j in range(n_blocks):
        ys.append(jnp.dot(x_ref[j, :, 0:nst], cre_ref[j], preferred_element_type=F32)
                  - jnp.dot(x_ref[j, :, nst:2 * nst], cim_ref[j], preferred_element_type=F32))
    y = jnp.concatenate(ys, axis=1) + dsk_ref[...] * u_ref[0].astype(F32)
    z = 0.5 * y * (1.0 + jnp.tanh(math.sqrt(2.0 / math.pi) * (y + 0.044715 * (y * y * y))))
    gl = jnp.dot(z.astype(BF16), wglu_ref[...], preferred_element_type=F32) + bglu_ref[...]
    zz = z * _sigmoid(gl)
    o_ref[0] = (zz * _silu(zs_ref[0].astype(F32))).astype(BF16)


def _s5_mixer(u, zs, tabs, c_re, c_im, d_skip, w_glu, b_glu):
    bsz, seq, width = u.shape
    ep_re, ep_im, em_re, em_im, laml_re, laml_im, bb_re, bb_im = tabs
    tm = min(SSM_TILE, seq)
    gpb = SSM_BLOCK_CH // SSM_GROUP
    n_blocks = width // SSM_BLOCK_CH
    nst = gpb * SSM_STATE
    eye = jnp.eye(gpb, dtype=F32)

    def b_block(t):
        t4 = t.reshape(n_blocks, gpb, SSM_STATE, SSM_GROUP)
        return jnp.einsum('jgpc,gh->jgchp', t4, eye).reshape(n_blocks, SSM_BLOCK_CH, nst)

    def c_block(t):
        t4 = t.reshape(n_blocks, gpb, SSM_GROUP, SSM_STATE)
        return jnp.einsum('jgcp,gh->jhpgc', t4, eye).reshape(n_blocks, nst, SSM_BLOCK_CH)

    bblk = jnp.concatenate([b_block(bb_re), b_block(bb_im)], axis=2).astype(BF16)
    cre = c_block(c_re).astype(BF16)
    cim = c_block(c_im).astype(BF16)
    t_idx = np.arange(tm)
    tri = jnp.asarray(((t_idx[:, None] >= t_idx[None, :])
                       & (t_idx[:, None] // SSM_CHUNK == t_idx[None, :] // SSM_CHUNK)
                       ).astype(np.float32), BF16)
    n = ep_re.shape[1]
    const = lambda shape: pl.BlockSpec(shape, lambda b, s: (0,) * len(shape))
    tok = pl.BlockSpec((1, tm, width), lambda b, s: (b, s, 0))
    kern = functools.partial(_s5_kernel, n_blocks=n_blocks)
    return pl.pallas_call(
        kern,
        out_shape=jax.ShapeDtypeStruct((bsz, seq, width), BF16),
        grid=(bsz, seq // tm),
        in_specs=[tok, tok,
                  const((n_blocks, SSM_BLOCK_CH, 2 * nst)),
                  const((n_blocks, nst, SSM_BLOCK_CH)), const((n_blocks, nst, SSM_BLOCK_CH)),
                  const((SSM_CHUNK, n)), const((SSM_CHUNK, n)),
                  const((SSM_CHUNK, n)), const((SSM_CHUNK, n)),
                  const((1, n)), const((1, n)), const((tm, tm)),
                  const((1, width)), const((width, width)), const((1, width))],
        out_specs=tok,
        scratch_shapes=[pltpu.VMEM((1, n), F32), pltpu.VMEM((1, n), F32),
                        pltpu.VMEM((n_blocks, tm, 2 * nst), BF16),
                        pltpu.VMEM((n_blocks, tm, 2 * nst), BF16)],
        compiler_params=_cparams(("parallel", "arbitrary")),
        name="s5_mixer",
    )(u, zs, bblk, cre, cim, ep_re.astype(BF16), ep_im.astype(BF16),
      em_re.astype(BF16), em_im.astype(BF16), laml_re, laml_im, tri,
      d_skip.reshape(1, width), w_glu.astype(BF16), b_glu.reshape(1, width))


def _dsa_kernel(k_ref, vt_ref, kidx_ref, qt_ref, qit_ref, azt_ref, wt_ref, tz_ref,
                o_ref, sc_ref, qip_ref, qp_ref, m_ref, l_ref, acc_ref, s_ref, s2_ref, mt_ref, mt2_ref, sel_ref,
                *, topk, n_heads):
    t = ATT_TILE
    qi = pl.program_id(1)
    nk = qi + 1
    seq = k_ref.shape[1]
    row = lax.broadcasted_iota(jnp.int32, (t, t), 0)
    colq = lax.broadcasted_iota(jnp.int32, (t, t), 1)
    lane = lax.broadcasted_iota(jnp.int32, (1, t), 1)
    pos = qi * t + lane
    kq = jnp.minimum(pos + 1, topk).astype(F32)

    def tile_rows(kt):
        return pl.ds(pl.multiple_of(kt * t, t), t)

    zpad = jnp.zeros((HEAD_DIM, t), BF16)
    for h in range(IDX_HEADS):
        qip_ref[h] = jnp.concatenate([qit_ref[0, h * IDX_DIM:(h + 1) * IDX_DIM, :], zpad], axis=0)
    for h in range(n_heads):
        qh = qt_ref[0, h * HEAD_DIM:(h + 1) * HEAD_DIM, :]
        qp_ref[h] = jnp.concatenate([qh, zpad] if h % 2 == 0 else [zpad, qh], axis=0)

    def score_tile(kt):
        kx = kidx_ref[0, tile_rows(kt), :]
        s = jnp.zeros((t, t), F32)
        for h in range(IDX_HEADS):
            rel = jnp.dot(kx, qip_ref[h], preferred_element_type=F32)
            s = s + wt_ref[0, h:h + 1, :] * jnp.maximum(rel, 0.0)
        return s

    def score_far(kt, carry):
        smin, smax = carry
        s = score_tile(kt)
        sc_ref[tile_rows(kt), :] = s
        return (jnp.minimum(smin, jnp.min(s, axis=0, keepdims=True)),
                jnp.maximum(smax, jnp.max(s, axis=0, keepdims=True)))

    smin, smax = lax.fori_loop(
        0, qi, score_far,
        (jnp.full((1, t), jnp.inf, F32), jnp.full((1, t), -jnp.inf, F32)))
    s = score_tile(qi)
    causal = row <= colq
    sc_ref[tile_rows(qi), :] = jnp.where(causal, s, -jnp.inf)
    smin = jnp.minimum(smin, jnp.min(jnp.where(causal, s, jnp.inf), axis=0, keepdims=True))
    smax = jnp.maximum(smax, jnp.max(jnp.where(causal, s, -jnp.inf), axis=0, keepdims=True))

    def count_where(pred_fn):
        def body(kt, acc):
            m = pred_fn(sc_ref[tile_rows(kt), :])
            return acc + jnp.sum(jnp.where(m, 1.0, 0.0).reshape(4, t // 32, 8, t), axis=1)
        acc = lax.fori_loop(0, nk, body, jnp.zeros((4, 8, t), F32))
        return jnp.sum(acc.reshape(32, t), axis=0, keepdims=True)

    def count_ge(thr):
        return count_where(lambda s_: s_ >= thr)

    def count_gt(thr):
        return count_where(lambda s_: s_ > thr)

    def min_where(pred_fn):
        def body(kt, acc):
            s = sc_ref[tile_rows(kt), :]
            return jnp.minimum(acc, jnp.min(jnp.where(pred_fn(s), s, jnp.inf), axis=0, keepdims=True))
        return lax.fori_loop(0, nk, body, jnp.full((1, t), jnp.inf, F32))

    def bisect(st):
        lo, hi, c_lo = st
        mid = lo + 0.5 * (hi - lo)
        cnt = count_ge(mid)
        up = cnt >= kq
        return jnp.where(up, mid, lo), jnp.where(up, hi, mid), jnp.where(up, cnt, c_lo)

    c_ge0 = count_ge(0.0)
    c_gt0 = count_gt(0.0)
    above = c_gt0 >= kq
    below = c_ge0 < kq
    at_zero = jnp.logical_not(above | below)
    zero_tie = at_zero & (c_ge0 > kq)
    lo_neg = above & (smin <= 0.0)
    lo0 = jnp.where(at_zero | lo_neg, 0.0, smin)
    c_lo0 = jnp.where(at_zero | lo_neg, c_ge0, (pos + 1).astype(F32))
    hi0 = jnp.where(at_zero, 0.0, jnp.where(below, jnp.minimum(smax, 0.0), smax))
    open_f = jnp.where(zero_tie, 0.0, 1.0)

    st = lax.fori_loop(0, BISECT_MIN_ITERS, lambda _, s_: bisect(s_), (lo0, hi0, c_lo0))

    def more_cond(c):
        return (c[3] > 0.0) & (c[4] < BISECT_MAX_ITERS)

    def more_body(c):
        st_ = (c[0], c[1], c[2])
        flag = jnp.max((c[2] - kq) * open_f)
        return bisect(st_) + (flag, c[4] + 1)

    lo, _, c_lo, _, _ = lax.while_loop(
        more_cond, more_body, st + (jnp.float32(1.0), jnp.int32(BISECT_MIN_ITERS)))

    sel_ref[0:1, :] = lo
    sel_ref[1:2, :] = jnp.zeros((1, t), F32)
    sel_ref[2:3, :] = kq - c_gt0
    sel_ref[3:4, :] = jnp.where(zero_tie, 1.0, 0.0)

    @pl.when(jnp.max((c_lo - kq) * open_f) > 0.0)
    def _():
        def peel_cond(c):
            return c[6] > 0.0

        def peel(c):
            lo_, c_lo_, taup_, ngt_, gtie_ = c[0], c[1], c[2], c[3], c[4]
            taup_n = min_where(lambda s_: s_ >= lo_)
            ngt_n = count_gt(taup_n)
            active = (c_lo_ > kq) & (open_f > 0.0) & (gtie_ == 0.0)
            is_tie = active & (ngt_n < kq)
            need_peel = active & (ngt_n >= kq)
            nxt = min_where(lambda s_: s_ > taup_n)
            return (jnp.where(need_peel, nxt, lo_), jnp.where(need_peel, ngt_n, c_lo_),
                    jnp.where(is_tie, taup_n, taup_), jnp.where(is_tie, ngt_n, ngt_),
                    jnp.where(is_tie, 1.0, gtie_), c[5],
                    jnp.max(jnp.where(need_peel, 1.0, 0.0)))

        zero = jnp.zeros((1, t), F32)
        lo2, _, taup2, ngt2, gtie2, _, _ = lax.while_loop(
            peel_cond, peel, (lo, c_lo, zero, zero, zero, zero, jnp.float32(1.0)))
        gt = gtie2 > 0.0
        sel_ref[0:1, :] = lo2
        sel_ref[1:2, :] = jnp.where(gt, taup2, sel_ref[1:2, :])
        sel_ref[2:3, :] = jnp.where(gt, kq - ngt2, sel_ref[2:3, :])
        sel_ref[3:4, :] = jnp.where(gt, 1.0, sel_ref[3:4, :])

    lo_f = sel_ref[0:1, :]
    tie_f = sel_ref[3:4, :]
    any_tie = jnp.max(tie_f)

    @pl.when(any_tie <= 0.0)
    def _():
        def body(kt, c):
            s_ = sc_ref[tile_rows(kt), :]
            sc_ref[tile_rows(kt), :] = jnp.where(s_ >= lo_f, 0.0, NEG)
            return c
        lax.fori_loop(0, nk, body, 0)

    @pl.when(any_tie > 0.0)
    def _():
        taup = sel_ref[1:2, :]
        need = jnp.where(tie_f > 0.0, sel_ref[2:3, :], float(2 * seq))
        tri = jnp.where(colq <= row, 1.0, 0.0).astype(BF16)

        def body(kt, seen):
            s_ = sc_ref[tile_rows(kt), :]
            z = s_ == taup
            pref = jnp.dot(tri, jnp.where(z, 1.0, 0.0).astype(BF16), preferred_element_type=F32)
            keep = (s_ >= lo_f) & jnp.logical_not(z & (seen + pref > need))
            sc_ref[tile_rows(kt), :] = jnp.where(keep, 0.0, NEG)
            return seen + pref[t - 1:t, :]
        lax.fori_loop(0, nk, body, jnp.zeros((1, t), F32))

    m_ref[...] = jnp.full_like(m_ref, NEG)
    l_ref[...] = jnp.zeros_like(l_ref)
    acc_ref[...] = jnp.zeros_like(acc_ref)

    def logits_stage(kt, s_buf, mt_buf):
        rows = tile_rows(kt)
        near = jnp.minimum(qi - kt, 2)
        for h in range(n_heads):
            pair = slice((h // 2) * 2 * HEAD_DIM, (h // 2 + 1) * 2 * HEAD_DIM)
            s = jnp.dot(k_ref[0, rows, pair], qp_ref[h], preferred_element_type=F32)
            s = s + sc_ref[rows, :] + tz_ref[near, h]
            s_buf[h] = s
            mt_buf[h:h + 1, :] = jnp.max(s, axis=0, keepdims=True)

    def softmax_stage(kt, s_buf, mt_buf):
        rows = tile_rows(kt)
        for h in range(n_heads):
            hs = slice(h * HEAD_DIM, (h + 1) * HEAD_DIM)
            m_old = m_ref[h:h + 1, :]
            m_new = jnp.maximum(m_old, mt_buf[h:h + 1, :])
            corr = jnp.exp2(m_old - m_new)
            p = jnp.exp2(s_buf[h] - m_new)
            l_ref[h:h + 1, :] = l_ref[h:h + 1, :] * corr + jnp.sum(p, axis=0, keepdims=True)
            acc_ref[hs, :] = acc_ref[hs, :] * corr + jnp.dot(
                vt_ref[0, hs, rows], p.astype(BF16), preferred_element_type=F32)
            m_ref[h:h + 1, :] = m_new

    logits_stage(0, s_ref, mt_ref)

    def kv_body(j, c):
        i = 2 * j
        logits_stage(jnp.minimum(i + 1, qi), s2_ref, mt2_ref)
        softmax_stage(i, s_ref, mt_ref)

        @pl.when(i + 1 < nk)
        def _():
            logits_stage(jnp.minimum(i + 2, qi), s_ref, mt_ref)
            softmax_stage(i + 1, s2_ref, mt2_ref)
        return c

    lax.fori_loop(0, (nk + 1) // 2, kv_body, 0)
    rcp = 1.0 / l_ref[...]
    rcp = jnp.broadcast_to(rcp[:, None, :], (n_heads, HEAD_DIM, t)).reshape(n_heads * HEAD_DIM, t)
    out = acc_ref[...] * rcp * _silu(azt_ref[0].astype(F32))
    o_ref[0] = out.T.astype(BF16)


def _dsa_mixer(k, vt, kidx, qt, qit, azt, wt, tz, topk):
    bsz, seq, width = k.shape
    n_heads = width // HEAD_DIM
    t = ATT_TILE
    per_b = lambda shape: pl.BlockSpec((1,) + shape, lambda b, q: (b, 0, 0))
    qtile = lambda r: pl.BlockSpec((1, r, t), lambda b, q: (b, 0, q))
    kern = functools.partial(_dsa_kernel, topk=topk, n_heads=n_heads)
    return pl.pallas_call(
        kern,
        out_shape=jax.ShapeDtypeStruct((bsz, seq, width), BF16),
        grid=(bsz, seq // t),
        in_specs=[per_b((seq, width)), per_b((width, seq)), per_b((seq, LANES)),
                  qtile(width), qtile(width), qtile(width), qtile(IDX_HEADS),
                  pl.BlockSpec((3, n_heads, t, t), lambda b, q: (0, 0, 0, 0))],
        out_specs=pl.BlockSpec((1, t, width), lambda b, q: (b, q, 0)),
        scratch_shapes=[pltpu.VMEM((seq, t), F32),
                        pltpu.VMEM((IDX_HEADS, 2 * IDX_DIM, t), BF16),
                        pltpu.VMEM((n_heads, 2 * HEAD_DIM, t), BF16),
                        pltpu.VMEM((n_heads, t), F32), pltpu.VMEM((n_heads, t), F32),
                        pltpu.VMEM((width, t), F32),
                        pltpu.VMEM((n_heads, t, t), F32), pltpu.VMEM((n_heads, t, t), F32),
                        pltpu.VMEM((n_heads, t), F32), pltpu.VMEM((n_heads, t), F32),
                        pltpu.VMEM((8, t), F32)],
        compiler_params=_cparams(("parallel", "arbitrary")),
        name="dsa_mixer",
    )(k, vt, kidx, qt, qit, azt, wt, tz)


def _out_kernel(x_ref, ys_ref, ya_ref, w1_ref, w2_ref, mod_ref, o_ref, *, d_model):
    gate = mod_ref[0, :, 2 * d_model:3 * d_model]
    y = (jnp.dot(ys_ref[0], w1_ref[...], preferred_element_type=F32)
         + jnp.dot(ya_ref[0], w2_ref[...], preferred_element_type=F32))
    o_ref[0] = x_ref[0] + gate * y


def _out_proj(x, y_ssm, y_attn, w_out, mod):
    bsz, seq, d = x.shape
    width = y_ssm.shape[-1]
    tm = min(OUT_TILE, seq)
    tok = lambda w: pl.BlockSpec((1, tm, w), lambda b, s: (b, s, 0))
    const = lambda shape: pl.BlockSpec(shape, lambda b, s: (0,) * len(shape))
    return pl.pallas_call(
        functools.partial(_out_kernel, d_model=d),
        out_shape=jax.ShapeDtypeStruct((bsz, seq, d), x.dtype),
        grid=(bsz, seq // tm),
        in_specs=[tok(d), tok(width), tok(width), const((width, d)), const((width, d)),
                  pl.BlockSpec((1, 1, 3 * d), lambda b, s: (b, 0, 0))],
        out_specs=tok(d),
        compiler_params=_cparams(("parallel", "arbitrary")),
        name="out_proj",
    )(x, y_ssm, y_attn, w_out[:width].astype(BF16), w_out[width:].astype(BF16),
      mod.reshape(bsz, 1, 3 * d))


def kernel(x, c, rel_bias, norm_g, w_ada, b_ada, w_in, q_gain, k_gain, a_re, a_im, log_dt,
           b_re, b_im, c_re, c_im, d_skip, w_glu, b_glu, w_out):
    bsz, seq, d = x.shape
    depth = w_in.shape[0]
    width = d // 2
    n_heads = width // HEAD_DIM
    topk = min(TOPK_MAX, seq // 4)
    assert seq % ATT_TILE == 0 and seq % SSM_CHUNK == 0
    assert ATT_TILE + 1 >= MAX_DISTANCE
    tz = _bias_prep(rel_bias, n_heads)
    for l in range(depth):
        mod = _adaln_mod(c, w_ada[l], b_ada[l])
        u, zs, k, kidx, qt, vt, qit, azt, wt = _in_proj(x, mod, norm_g[l], w_in[l],
                                                        q_gain[l], k_gain[l])
        tabs = _ssm_prep(a_re[l], a_im[l], log_dt[l], b_re[l], b_im[l])
        y_ssm = _s5_mixer(u, zs, tabs, c_re[l], c_im[l], d_skip[l], w_glu[l], b_glu[l])
        y_attn = _dsa_mixer(k, vt, kidx, qt, qit, azt, wt, tz, topk)
        x = _out_proj(x, y_ssm, y_attn, w_out[l], mod)
    return x
```

```python
import functools
import math

import jax
import jax.numpy as jnp
import numpy as np
from jax import lax
from jax.experimental import pallas as pl
from jax.experimental.pallas import tpu as pltpu

F32 = jnp.float32
BF16 = jnp.bfloat16

SSM_GROUP = 16
SSM_STATE = 64
HEAD_DIM = 64
VAUG = 16
IDX_HEADS = 8
IDX_DIM = 64
TOPK_MAX = 256
NUM_BUCKETS = 32
MAX_DISTANCE = 128
EPS = 1e-6

LANES = 128
V7X_VMEM_LIMIT_BYTES = 56 * 1024 * 1024

SSM_CHUNK = 128
SSM_TILE = 256
SSM_BLOCK_CH = 128
PROJ_TILE = 512
ATT_TILE = 256
ATT_QUERIES = 512
OUT_TILE = 512
BISECT_MIN_ITERS = 14
BISECT_MAX_ITERS = 26
NEG = -1e30
LOG2E = math.log2(math.e)

NT_DIMS = (((1,), (1,)), ((), ()))


def _cparams(sem):
    return pltpu.CompilerParams(dimension_semantics=sem,
                                vmem_limit_bytes=V7X_VMEM_LIMIT_BYTES)


def _sigmoid(x):
    return 1.0 / (1.0 + jnp.exp(-x))


def _silu(x):
    return x * _sigmoid(x)


def _ssm_prep_kernel(are_r, aim_r, ldt_r, are_c, aim_c, ldt_c, bre, bim,
                     ep_re, ep_im, em_re, em_im, laml_re, laml_im, bb_re, bb_im):
    dt = jnp.exp(ldt_r[...])
    ar = are_r[...] * dt
    ai = aim_r[...] * dt
    n = ar.shape[-1]
    tau = lax.broadcasted_iota(jnp.int32, (SSM_CHUNK, n), 0).astype(F32)
    mag_p = jnp.exp(tau * ar)
    mag_m = jnp.exp(-tau * ar)
    cs = jnp.cos(tau * ai)
    sn = jnp.sin(tau * ai)
    ep_re[...] = mag_p * cs
    ep_im[...] = mag_p * sn
    em_re[...] = mag_m * cs
    em_im[...] = -(mag_m * sn)
    mag_l = jnp.exp(float(SSM_CHUNK) * ar)
    laml_re[...] = mag_l * jnp.cos(float(SSM_CHUNK) * ai)
    laml_im[...] = mag_l * jnp.sin(float(SSM_CHUNK) * ai)
    a = are_c[...]
    b = aim_c[...]
    dtc = jnp.exp(ldt_c[...])
    mag = jnp.exp(a * dtc)
    x = mag * jnp.cos(b * dtc) - 1.0
    y = mag * jnp.sin(b * dtc)
    den = a * a + b * b
    cre = (x * a + y * b) / den
    cim = (y * a - x * b) / den
    bb_re[...] = cre * bre[...] - cim * bim[...]
    bb_im[...] = cre * bim[...] + cim * bre[...]


def _ssm_prep(a_re, a_im, log_dt, b_re, b_im):
    g, p = a_re.shape
    n = g * p
    ldt = jnp.broadcast_to(log_dt[:, None], (g, p))
    row = lambda t: t.reshape(1, n)
    col = lambda t: t.reshape(n, 1)
    tab = jax.ShapeDtypeStruct((SSM_CHUNK, n), F32)
    vec = jax.ShapeDtypeStruct((1, n), F32)
    bsh = jax.ShapeDtypeStruct((n, SSM_GROUP), F32)
    return pl.pallas_call(
        _ssm_prep_kernel,
        out_shape=(tab, tab, tab, tab, vec, vec, bsh, bsh),
        name="ssm_prep",
    )(row(a_re), row(a_im), row(ldt), col(a_re), col(a_im), col(ldt),
      b_re.reshape(n, SSM_GROUP), b_im.reshape(n, SSM_GROUP))


def _bias_prep_kernel(rb_ref, o_ref):
    t = o_ref.shape[-1]
    n_heads = o_ref.shape[1]
    i = lax.broadcasted_iota(jnp.int32, (t, t), 0)
    j = lax.broadcasted_iota(jnp.int32, (t, t), 1)
    max_exact = NUM_BUCKETS // 2

    def bias(idx, h):
        dist = jnp.maximum(idx * t + j - i, 0)
        d = jnp.maximum(dist, 1).astype(F32)
        large = max_exact + (jnp.log(d / max_exact) / math.log(MAX_DISTANCE / max_exact)
                             * (NUM_BUCKETS - max_exact)).astype(jnp.int32)
        large = jnp.minimum(large, NUM_BUCKETS - 1)
        bucket = jnp.where(dist < max_exact, dist, large)
        acc = jnp.zeros((t, t), F32)
        for b in range(NUM_BUCKETS):
            acc = jnp.where(bucket == b, rb_ref[b, h], acc)
        return acc

    for h in range(n_heads):
        far = bias(2, h)
        for idx in range(3):
            o_ref[idx, h] = (bias(idx, h) - far) * LOG2E


def _bias_prep(rel_bias, n_heads):
    t = ATT_TILE
    return pl.pallas_call(
        _bias_prep_kernel,
        out_shape=jax.ShapeDtypeStruct((3, n_heads, t, t), F32),
        in_specs=[pl.BlockSpec(memory_space=pltpu.SMEM)],
        name="bias_prep",
    )(rel_bias)


def _mod_kernel(c_ref, w_ref, b_ref, o_ref):
    c = c_ref[...]
    cond = _silu(c)
    o_ref[...] = jnp.dot(cond.astype(BF16), w_ref[...].astype(BF16),
                         preferred_element_type=F32) + b_ref[...]


def _adaln_mod(c, w_ada, b_ada):
    bsz, d = c.shape
    n = w_ada.shape[1]
    tn = 512
    return pl.pallas_call(
        _mod_kernel,
        out_shape=jax.ShapeDtypeStruct((bsz, n), F32),
        grid=(n // tn,),
        in_specs=[pl.BlockSpec((bsz, d), lambda i: (0, 0)),
                  pl.BlockSpec((d, tn), lambda i: (0, i)),
                  pl.BlockSpec((1, tn), lambda i: (0, i))],
        out_specs=pl.BlockSpec((bsz, tn), lambda i: (0, i)),
        compiler_params=_cparams(("arbitrary",)),
        name="adaln_mod",
    )(c, w_ada, b_ada.reshape(1, n))


def _in_proj_kernel(x_ref, mod_ref, g_ref, wnat_ref, wt_ref, ww_ref, mblk_ref,
                    kg_ref, qg_ref,
                    u_ref, zs_ref, k_ref, kidx_ref, qt_ref, vt_ref, qit_ref, azt_ref, wt_out_ref,
                    *, d_model, width, n_heads):
    x = x_ref[0]
    shift = mod_ref[0, :, 0:d_model]
    scale = mod_ref[0, :, d_model:2 * d_model]
    ms = jnp.mean(x * x, axis=-1, keepdims=True)
    xn = x * lax.rsqrt(ms + EPS) * g_ref[...]
    h = (xn * (1.0 + scale) + shift).astype(BF16)
    tm = h.shape[0]

    nat = jnp.dot(h, wnat_ref[...], preferred_element_type=F32)
    u_ref[0] = nat[:, 0:width].astype(BF16)
    zs_ref[0] = nat[:, width:2 * width].astype(BF16)
    k = nat[:, 2 * width:3 * width]
    kms = jnp.dot((k * k).astype(BF16), mblk_ref[...], preferred_element_type=F32)
    k_ref[0] = (k * lax.rsqrt(kms + EPS) * kg_ref[...]).astype(BF16)
    kidx_ref[0] = nat[:, 3 * width:3 * width + LANES].astype(BF16)

    tr = lax.dot_general(wt_ref[...], h, NT_DIMS, preferred_element_type=F32)
    q3 = tr[0:width].reshape(n_heads, HEAD_DIM, tm)
    qms = jnp.mean(q3 * q3, axis=1, keepdims=True)
    qn = q3 * lax.rsqrt(qms + EPS) * qg_ref[...][None] * (HEAD_DIM ** -0.5 * LOG2E)
    qt_ref[0] = qn.reshape(width, tm).astype(BF16)
    v3 = tr[width:2 * width].astype(BF16).reshape(n_heads, HEAD_DIM, tm)
    aug = lax.broadcasted_iota(jnp.int32, (n_heads, VAUG, tm), 1)
    aug = jnp.where(aug == 0, 1.0, 0.0).astype(BF16)
    vt_ref[0] = jnp.concatenate([v3, aug], axis=1).reshape(n_heads * (HEAD_DIM + VAUG), tm)
    qit_ref[0] = tr[2 * width:3 * width].astype(BF16)
    azt_ref[0] = tr[3 * width:4 * width].astype(BF16)
    wt_out_ref[0] = lax.dot_general(ww_ref[...], h, NT_DIMS,
                                    preferred_element_type=F32) * (IDX_HEADS ** -0.5)


def _in_proj(x, mod, norm_g, w_in, q_gain, k_gain):
    bsz, seq, d = x.shape
    width = d // 2
    n_heads = width // HEAD_DIM
    tm = min(PROJ_TILE, seq)
    o = np.cumsum([0, width, width, width, width, width, width,
                   IDX_HEADS * IDX_DIM, IDX_DIM, IDX_HEADS])
    col = lambda i: w_in[:, int(o[i]):int(o[i + 1])]
    w_kidx = jnp.pad(col(7), ((0, 0), (0, LANES - IDX_DIM)))
    w_nat = jnp.concatenate([col(0), col(1), col(3), w_kidx], axis=1).astype(BF16)
    w_t = jnp.concatenate([col(2), col(4), col(6), col(5)], axis=1).T.astype(BF16)
    w_w = col(8).T.astype(BF16)
    hid = np.arange(width) // HEAD_DIM
    mblk = jnp.asarray((hid[:, None] == hid[None, :]).astype(np.float32) / HEAD_DIM, BF16)
    kg = jnp.tile(k_gain, n_heads).reshape(1, width)
    qg = q_gain.reshape(HEAD_DIM, 1)

    nat_w = w_nat.shape[1]
    bs = jax.ShapeDtypeStruct
    const = lambda shape: pl.BlockSpec(shape, lambda b, s: (0,) * len(shape))
    tok = lambda w: pl.BlockSpec((1, tm, w), lambda b, s: (b, s, 0))
    trn = lambda r: pl.BlockSpec((1, r, tm), lambda b, s: (b, 0, s))
    kern = functools.partial(_in_proj_kernel, d_model=d, width=width, n_heads=n_heads)
    vrows = n_heads * (HEAD_DIM + VAUG)
    return pl.pallas_call(
        kern,
        out_shape=(bs((bsz, seq, width), BF16), bs((bsz, seq, width), BF16),
                   bs((bsz, seq, width), BF16), bs((bsz, seq, LANES), BF16),
                   bs((bsz, width, seq), BF16), bs((bsz, vrows, seq), BF16),
                   bs((bsz, width, seq), BF16), bs((bsz, width, seq), BF16),
                   bs((bsz, IDX_HEADS, seq), F32)),
        grid=(bsz, seq // tm),
        in_specs=[tok(d),
                  pl.BlockSpec((1, 1, 3 * d), lambda b, s: (b, 0, 0)),
                  const((1, d)), const((d, nat_w)), const((4 * width, d)),
                  const((IDX_HEADS, d)), const((width, width)),
                  const((1, width)), const((HEAD_DIM, 1))],
        out_specs=(tok(width), tok(width), tok(width), tok(LANES),
                   trn(width), trn(vrows), trn(width), trn(width), trn(IDX_HEADS)),
        compiler_params=_cparams(("parallel", "arbitrary")),
        name="in_proj",
    )(x, mod.reshape(bsz, 1, 3 * d), norm_g.reshape(1, d), w_nat, w_t, w_w, mblk, kg, qg)


def _s5_kernel(u_ref, zs_ref, bblk_ref, cre_ref, cim_ref, ep_re, ep_im, em_re, em_im,
               laml_re, laml_im, tri_ref, dsk_ref, wglu_ref, bglu_ref,
               o_ref, car_re, car_im, w_ref, x_ref, *, n_blocks):
    tm = u_ref.shape[1]
    nch = tm // SSM_CHUNK
    nst = cre_ref.shape[1]

    @pl.when(pl.program_id(1) == 0)
    def _():
        car_re[...] = jnp.zeros_like(car_re)
        car_im[...] = jnp.zeros_like(car_im)

    tri = tri_ref[...]
    for j in range(n_blocks):
        st = slice(j * nst, (j + 1) * nst)
        ch = slice(j * SSM_BLOCK_CH, (j + 1) * SSM_BLOCK_CH)
        p = jnp.dot(u_ref[0, :, ch], bblk_ref[j], preferred_element_type=F32)
        p_re = p[:, :nst].astype(BF16).reshape(nch, SSM_CHUNK, nst)
        p_im = p[:, nst:].astype(BF16).reshape(nch, SSM_CHUNK, nst)
        emr = em_re[:, st][None]
        emi = em_im[:, st][None]
        w_ref[j, :, 0:nst] = (p_re * emr - p_im * emi).reshape(tm, nst)
        w_ref[j, :, nst:2 * nst] = (p_re * emi + p_im * emr).reshape(tm, nst)
    for j in range(n_blocks):
        st = slice(j * nst, (j + 1) * nst)
        cum = jnp.dot(tri, w_ref[j], preferred_element_type=F32)
        cum_re = cum[:, :nst].reshape(nch, SSM_CHUNK, nst)
        cum_im = cum[:, nst:].reshape(nch, SSM_CHUNK, nst)
        lr = laml_re[:, st]
        li = laml_im[:, st]
        cr = car_re[:, st]
        ci = car_im[:, st]
        crs, cis = [], []
        for c in range(nch):
            crs.append(cr)
            cis.append(ci)
            tr_ = cum_re[c, SSM_CHUNK - 1:SSM_CHUNK, :] + cr
            ti_ = cum_im[c, SSM_CHUNK - 1:SSM_CHUNK, :] + ci
            cr = lr * tr_ - li * ti_
            ci = lr * ti_ + li * tr_
        car_re[:, st] = cr
        car_im[:, st] = ci
        a_re = (cum_re + jnp.stack(crs, axis=0)).astype(BF16)
        a_im = (cum_im + jnp.stack(cis, axis=0)).astype(BF16)
        epr = ep_re[:, st][None]
        epi = ep_im[:, st][None]
        x_ref[j, :, 0:nst] = (a_re * epr - a_im * epi).reshape(tm, nst)
        x_ref[j, :, nst:2 * nst] = (a_re * epi + a_im * epr).reshape(tm, nst)
    ys = []
    for j in range(n_blocks):
        ys.append(jnp.dot(x_ref[j, :, 0:nst], cre_ref[j], preferred_element_type=F32)
                  - jnp.dot(x_ref[j, :, nst:2 * nst], cim_ref[j], preferred_element_type=F32))
    y = jnp.concatenate(ys, axis=1) + dsk_ref[...] * u_ref[0].astype(F32)
    z = 0.5 * y * (1.0 + jnp.tanh(math.sqrt(2.0 / math.pi) * (y + 0.044715 * (y * y * y))))
    gl = jnp.dot(z.astype(BF16), wglu_ref[...], preferred_element_type=F32) + bglu_ref[...]
    zz = z * _sigmoid(gl)
    o_ref[0] = (zz * _silu(zs_ref[0].astype(F32))).astype(BF16)


def _s5_mixer(u, zs, tabs, c_re, c_im, d_skip, w_glu, b_glu):
    bsz, seq, width = u.shape
    ep_re, ep_im, em_re, em_im, laml_re, laml_im, bb_re, bb_im = tabs
    tm = min(SSM_TILE, seq)
    gpb = SSM_BLOCK_CH // SSM_GROUP
    n_blocks = width // SSM_BLOCK_CH
    nst = gpb * SSM_STATE
    eye = jnp.eye(gpb, dtype=F32)

    def b_block(t):
        t4 = t.reshape(n_blocks, gpb, SSM_STATE, SSM_GROUP)
        return jnp.einsum('jgpc,gh->jgchp', t4, eye).reshape(n_blocks, SSM_BLOCK_CH, nst)

    def c_block(t):
        t4 = t.reshape(n_blocks, gpb, SSM_GROUP, SSM_STATE)
        return jnp.einsum('jgcp,gh->jhpgc', t4, eye).reshape(n_blocks, nst, SSM_BLOCK_CH)

    bblk = jnp.concatenate([b_block(bb_re), b_block(bb_im)], axis=2).astype(BF16)
    cre = c_block(c_re).astype(BF16)
    cim = c_block(c_im).astype(BF16)
    t_idx = np.arange(tm)
    tri = jnp.asarray(((t_idx[:, None] >= t_idx[None, :])
                       & (t_idx[:, None] // SSM_CHUNK == t_idx[None, :] // SSM_CHUNK)
                       ).astype(np.float32), BF16)
    n = ep_re.shape[1]
    const = lambda shape: pl.BlockSpec(shape, lambda b, s: (0,) * len(shape))
    tok = pl.BlockSpec((1, tm, width), lambda b, s: (b, s, 0))
    kern = functools.partial(_s5_kernel, n_blocks=n_blocks)
    return pl.pallas_call(
        kern,
        out_shape=jax.ShapeDtypeStruct((bsz, seq, width), BF16),
        grid=(bsz, seq // tm),
        in_specs=[tok, tok,
                  const((n_blocks, SSM_BLOCK_CH, 2 * nst)),
                  const((n_blocks, nst, SSM_BLOCK_CH)), const((n_blocks, nst, SSM_BLOCK_CH)),
                  const((SSM_CHUNK, n)), const((SSM_CHUNK, n)),
                  const((SSM_CHUNK, n)), const((SSM_CHUNK, n)),
                  const((1, n)), const((1, n)), const((tm, tm)),
                  const((1, width)), const((width, width)), const((1, width))],
        out_specs=tok,
        scratch_shapes=[pltpu.VMEM((1, n), F32), pltpu.VMEM((1, n), F32),
                        pltpu.VMEM((n_blocks, tm, 2 * nst), BF16),
                        pltpu.VMEM((n_blocks, tm, 2 * nst), BF16)],
        compiler_params=_cparams(("parallel", "arbitrary")),
        name="s5_mixer",
    )(u, zs, bblk, cre, cim, ep_re.astype(BF16), ep_im.astype(BF16),
      em_re.astype(BF16), em_im.astype(BF16), laml_re, laml_im, tri,
      d_skip.reshape(1, width), w_glu.astype(BF16), b_glu.reshape(1, width))


def _dsa_kernel(k_ref, vt_ref, kidx_ref, qt_ref, qit_ref, azt_ref, wt_ref, tz_ref,
                o_ref, sc_ref, qip_ref, qp_ref, m_ref, acc_ref, s_ref, s2_ref, mt_ref, mt2_ref,
                p_ref, p2_ref, c_ref, c2_ref, sel_ref,
                *, topk, n_heads):
    t = ATT_TILE
    tq = qt_ref.shape[2]
    kpq = tq // t
    qi = pl.program_id(1)
    n_far = qi * kpq
    nk = n_far + kpq
    seq = k_ref.shape[1]
    row = lax.broadcasted_iota(jnp.int32, (t, tq), 0)
    colq = lax.broadcasted_iota(jnp.int32, (t, tq), 1)
    lane = lax.broadcasted_iota(jnp.int32, (1, tq), 1)
    pos = qi * tq + lane
    kq = jnp.minimum(pos + 1, topk).astype(F32)

    def tile_rows(kt):
        return pl.ds(pl.multiple_of(kt * t, t), t)

    zpad = jnp.zeros((HEAD_DIM, tq), BF16)
    for h in range(IDX_HEADS):
        qip_ref[h] = jnp.concatenate([qit_ref[0, h * IDX_DIM:(h + 1) * IDX_DIM, :], zpad], axis=0)
    for h in range(n_heads):
        qh = qt_ref[0, h * HEAD_DIM:(h + 1) * HEAD_DIM, :]
        qp_ref[h] = jnp.concatenate([qh, zpad] if h % 2 == 0 else [zpad, qh], axis=0)

    def score_tile(kt):
        kx = kidx_ref[0, tile_rows(kt), :]
        s = jnp.zeros((t, tq), F32)
        for h in range(IDX_HEADS):
            rel = jnp.dot(kx, qip_ref[h], preferred_element_type=F32)
            s = s + wt_ref[0, h:h + 1, :] * jnp.maximum(rel, 0.0)
        return s

    def score_far(kt, carry):
        smin, smax = carry
        s = score_tile(kt)
        sc_ref[tile_rows(kt), :] = s
        return (jnp.minimum(smin, jnp.min(s, axis=0, keepdims=True)),
                jnp.maximum(smax, jnp.max(s, axis=0, keepdims=True)))

    smin, smax = lax.fori_loop(
        0, n_far, score_far,
        (jnp.full((1, tq), jnp.inf, F32), jnp.full((1, tq), -jnp.inf, F32)))
    for d in range(kpq):
        s = score_tile(n_far + d)
        causal = (row + d * t) <= colq
        sc_ref[tile_rows(n_far + d), :] = jnp.where(causal, s, -jnp.inf)
        smin = jnp.minimum(smin, jnp.min(jnp.where(causal, s, jnp.inf), axis=0, keepdims=True))
        smax = jnp.maximum(smax, jnp.max(jnp.where(causal, s, -jnp.inf), axis=0, keepdims=True))

    def count_where(pred_fn):
        def body(kt, acc):
            m = pred_fn(sc_ref[tile_rows(kt), :])
            return acc + jnp.sum(jnp.where(m, 1.0, 0.0).reshape(4, t // 32, 8, tq), axis=1)
        acc = lax.fori_loop(0, nk, body, jnp.zeros((4, 8, tq), F32))
        return jnp.sum(acc.reshape(32, tq), axis=0, keepdims=True)

    def count_ge(thr):
        return count_where(lambda s_: s_ >= thr)

    def count_gt(thr):
        return count_where(lambda s_: s_ > thr)

    def min_where(pred_fn):
        def body(kt, acc):
            s = sc_ref[tile_rows(kt), :]
            return jnp.minimum(acc, jnp.min(jnp.where(pred_fn(s), s, jnp.inf), axis=0, keepdims=True))
        return lax.fori_loop(0, nk, body, jnp.full((1, tq), jnp.inf, F32))

    def bisect(st):
        lo, hi, c_lo = st
        mid = lo + 0.5 * (hi - lo)
        cnt = count_ge(mid)
        up = cnt >= kq
        return jnp.where(up, mid, lo), jnp.where(up, hi, mid), jnp.where(up, cnt, c_lo)

    c_ge0 = count_ge(0.0)
    c_gt0 = count_gt(0.0)
    above = c_gt0 >= kq
    below = c_ge0 < kq
    at_zero = jnp.logical_not(above | below)
    zero_tie = at_zero & (c_ge0 > kq)
    lo_neg = above & (smin <= 0.0)
    lo0 = jnp.where(at_zero | lo_neg, 0.0, smin)
    c_lo0 = jnp.where(at_zero | lo_neg, c_ge0, (pos + 1).astype(F32))
    hi0 = jnp.where(at_zero, 0.0, jnp.where(below, jnp.minimum(smax, 0.0), smax))
    open_f = jnp.where(zero_tie, 0.0, 1.0)

    st = lax.fori_loop(0, BISECT_MIN_ITERS, lambda _, s_: bisect(s_), (lo0, hi0, c_lo0))

    def more_cond(c):
        return (c[3] > 0.0) & (c[4] < BISECT_MAX_ITERS)

    def more_body(c):
        st_ = (c[0], c[1], c[2])
        flag = jnp.max((c[2] - kq) * open_f)
        return bisect(st_) + (flag, c[4] + 1)

    lo, _, c_lo, _, _ = lax.while_loop(
        more_cond, more_body, st + (jnp.float32(1.0), jnp.int32(BISECT_MIN_ITERS)))

    sel_ref[0:1, :] = lo
    sel_ref[1:2, :] = jnp.zeros((1, tq), F32)
    sel_ref[2:3, :] = kq - c_gt0
    sel_ref[3:4, :] = jnp.where(zero_tie, 1.0, 0.0)

    @pl.when(jnp.max((c_lo - kq) * open_f) > 0.0)
    def _():
        def peel_cond(c):
            return c[6] > 0.0

        def peel(c):
            lo_, c_lo_, taup_, ngt_, gtie_ = c[0], c[1], c[2], c[3], c[4]
            taup_n = min_where(lambda s_: s_ >= lo_)
            ngt_n = count_gt(taup_n)
            active = (c_lo_ > kq) & (open_f > 0.0) & (gtie_ == 0.0)
            is_tie = active & (ngt_n < kq)
            need_peel = active & (ngt_n >= kq)
            nxt = min_where(lambda s_: s_ > taup_n)
            return (jnp.where(need_peel, nxt, lo_), jnp.where(need_peel, ngt_n, c_lo_),
                    jnp.where(is_tie, taup_n, taup_), jnp.where(is_tie, ngt_n, ngt_),
                    jnp.where(is_tie, 1.0, gtie_), c[5],
                    jnp.max(jnp.where(need_peel, 1.0, 0.0)))

        zero = jnp.zeros((1, tq), F32)
        lo2, _, taup2, ngt2, gtie2, _, _ = lax.while_loop(
            peel_cond, peel, (lo, c_lo, zero, zero, zero, zero, jnp.float32(1.0)))
        gt = gtie2 > 0.0
        sel_ref[0:1, :] = lo2
        sel_ref[1:2, :] = jnp.where(gt, taup2, sel_ref[1:2, :])
        sel_ref[2:3, :] = jnp.where(gt, kq - ngt2, sel_ref[2:3, :])
        sel_ref[3:4, :] = jnp.where(gt, 1.0, sel_ref[3:4, :])

    lo_f = sel_ref[0:1, :]
    tie_f = sel_ref[3:4, :]
    any_tie = jnp.max(tie_f)

    @pl.when(any_tie <= 0.0)
    def _():
        def body(kt, c):
            s_ = sc_ref[tile_rows(kt), :]
            sc_ref[tile_rows(kt), :] = jnp.where(s_ >= lo_f, 0.0, NEG)
            return c
        lax.fori_loop(0, nk, body, 0)

    @pl.when(any_tie > 0.0)
    def _():
        taup = sel_ref[1:2, :]
        need = jnp.where(tie_f > 0.0, sel_ref[2:3, :], float(2 * seq))
        tr_ = lax.broadcasted_iota(jnp.int32, (t, t), 0)
        tc_ = lax.broadcasted_iota(jnp.int32, (t, t), 1)
        tri = jnp.where(tc_ <= tr_, 1.0, 0.0).astype(BF16)

        def body(kt, seen):
            s_ = sc_ref[tile_rows(kt), :]
            z = s_ == taup
            pref = jnp.dot(tri, jnp.where(z, 1.0, 0.0).astype(BF16), preferred_element_type=F32)
            keep = (s_ >= lo_f) & jnp.logical_not(z & (seen + pref > need))
            sc_ref[tile_rows(kt), :] = jnp.where(keep, 0.0, NEG)
            return seen + pref[t - 1:t, :]
        lax.fori_loop(0, nk, body, jnp.zeros((1, tq), F32))

    va = HEAD_DIM + VAUG

    def logits_head(h, kt, s_buf, mt_buf):
        rows = tile_rows(kt)
        pair = slice((h // 2) * 2 * HEAD_DIM, (h // 2 + 1) * 2 * HEAD_DIM)
        s = jnp.dot(k_ref[0, rows, pair], qp_ref[h], preferred_element_type=F32)
        s = s + sc_ref[rows, :]
        parts = []
        for g in range(tq // t):
            near = jnp.clip(n_far + g - kt, 0, 2)
            parts.append(s[:, g * t:(g + 1) * t] + tz_ref[near, h])
        s = jnp.concatenate(parts, axis=1) if len(parts) > 1 else parts[0]
        s_buf[h] = s
        mt_buf[h:h + 1, :] = jnp.max(s, axis=0, keepdims=True)

    def softmax_head(h, s_buf, mt_buf, p_buf, corr_buf):
        m_old = m_ref[h:h + 1, :]
        m_new = jnp.maximum(m_old, mt_buf[h:h + 1, :])
        corr_buf[h:h + 1, :] = jnp.exp2(m_old - m_new)
        p_buf[h] = jnp.exp2(s_buf[h] - m_new).astype(BF16)
        m_ref[h:h + 1, :] = m_new

    def pv_head(h, kt, p_buf, corr_buf):
        rows = tile_rows(kt)
        hs = slice(h * va, (h + 1) * va)
        acc_ref[hs, :] = acc_ref[hs, :] * corr_buf[h:h + 1, :] + jnp.dot(
            vt_ref[0, hs, rows], p_buf[h], preferred_element_type=F32)

    def logits_stage(kt, s_buf, mt_buf):
        for h in range(n_heads):
            logits_head(h, kt, s_buf, mt_buf)

    def pv_stage(kt, p_buf, corr_buf):
        for h in range(n_heads):
            pv_head(h, kt, p_buf, corr_buf)

    def trip(kt_pv, p_old, c_old, kt_next, s_next, mt_next, s_cur, mt_cur, p_cur, c_cur):
        for h in range(n_heads):
            pv_head(h, kt_pv, p_old, c_old)
            logits_head(h, kt_next, s_next, mt_next)
            softmax_head(h, s_cur, mt_cur, p_cur, c_cur)

    m_ref[...] = jnp.full_like(m_ref, NEG)
    acc_ref[...] = jnp.zeros_like(acc_ref)
    p2_ref[...] = jnp.zeros_like(p2_ref)
    c2_ref[...] = jnp.ones_like(c2_ref)
    logits_stage(0, s_ref, mt_ref)
    last = nk - 1

    def kv_body(j, c):
        i = 2 * j
        trip(jnp.maximum(i - 1, 0), p2_ref, c2_ref,
             jnp.minimum(i + 1, last), s2_ref, mt2_ref,
             s_ref, mt_ref, p_ref, c_ref)

        @pl.when(i + 1 < nk)
        def _():
            trip(i, p_ref, c_ref,
                 jnp.minimum(i + 2, last), s_ref, mt_ref,
                 s2_ref, mt2_ref, p2_ref, c2_ref)
        return c

    lax.fori_loop(0, (nk + 1) // 2, kv_body, 0)

    if kpq % 2 == 0:
        pv_stage(last, p2_ref, c2_ref)
    else:
        @pl.when(jnp.bitwise_and(nk, 1) == 1)
        def _():
            pv_stage(last, p_ref, c_ref)

        @pl.when(jnp.bitwise_and(nk, 1) == 0)
        def _():
            pv_stage(last, p2_ref, c2_ref)

    acc = acc_ref[...].reshape(n_heads, va, tq)
    out = acc[:, :HEAD_DIM, :] * (1.0 / acc[:, HEAD_DIM:HEAD_DIM + 1, :])
    out = out.reshape(n_heads * HEAD_DIM, tq) * _silu(azt_ref[0].astype(F32))
    o_ref[0] = out.T.astype(BF16)


def _dsa_mixer(k, vt, kidx, qt, qit, azt, wt, tz, topk):
    bsz, seq, width = k.shape
    n_heads = width // HEAD_DIM
    t = ATT_TILE
    tq = min(ATT_QUERIES, seq)
    per_b = lambda shape: pl.BlockSpec((1,) + shape, lambda b, q: (b, 0, 0))
    qtile = lambda r: pl.BlockSpec((1, r, tq), lambda b, q: (b, 0, q))
    kern = functools.partial(_dsa_kernel, topk=topk, n_heads=n_heads)
    vrows = vt.shape[1]
    return pl.pallas_call(
        kern,
        out_shape=jax.ShapeDtypeStruct((bsz, seq, width), BF16),
        grid=(bsz, seq // tq),
        in_specs=[per_b((seq, width)), per_b((vrows, seq)), per_b((seq, LANES)),
                  qtile(width), qtile(width), qtile(width), qtile(IDX_HEADS),
                  pl.BlockSpec((3, n_heads, t, t), lambda b, q: (0, 0, 0, 0))],
        out_specs=pl.BlockSpec((1, tq, width), lambda b, q: (b, q, 0)),
        scratch_shapes=[pltpu.VMEM((seq, tq), F32),
                        pltpu.VMEM((IDX_HEADS, 2 * IDX_DIM, tq), BF16),
                        pltpu.VMEM((n_heads, 2 * HEAD_DIM, tq), BF16),
                        pltpu.VMEM((n_heads, tq), F32),
                        pltpu.VMEM((vrows, tq), F32),
                        pltpu.VMEM((n_heads, t, tq), F32), pltpu.VMEM((n_heads, t, tq), F32),
                        pltpu.VMEM((n_heads, tq), F32), pltpu.VMEM((n_heads, tq), F32),
                        pltpu.VMEM((n_heads, t, tq), BF16), pltpu.VMEM((n_heads, t, tq), BF16),
                        pltpu.VMEM((n_heads, tq), F32), pltpu.VMEM((n_heads, tq), F32),
                        pltpu.VMEM((8, tq), F32)],
        compiler_params=_cparams(("parallel", "arbitrary")),
        name="dsa_mixer",
    )(k, vt, kidx, qt, qit, azt, wt, tz)


def _out_kernel(x_ref, ys_ref, ya_ref, w1_ref, w2_ref, mod_ref, o_ref, *, d_model):
    gate = mod_ref[0, :, 2 * d_model:3 * d_model]
    y = (jnp.dot(ys_ref[0], w1_ref[...], preferred_element_type=F32)
         + jnp.dot(ya_ref[0], w2_ref[...], preferred_element_type=F32))
    o_ref[0] = x_ref[0] + gate * y


def _out_proj(x, y_ssm, y_attn, w_out, mod):
    bsz, seq, d = x.shape
    width = y_ssm.shape[-1]
    tm = min(OUT_TILE, seq)
    tok = lambda w: pl.BlockSpec((1, tm, w), lambda b, s: (b, s, 0))
    const = lambda shape: pl.BlockSpec(shape, lambda b, s: (0,) * len(shape))
    return pl.pallas_call(
        functools.partial(_out_kernel, d_model=d),
        out_shape=jax.ShapeDtypeStruct((bsz, seq, d), x.dtype),
        grid=(bsz, seq // tm),
        in_specs=[tok(d), tok(width), tok(width), const((width, d)), const((width, d)),
                  pl.BlockSpec((1, 1, 3 * d), lambda b, s: (b, 0, 0))],
        out_specs=tok(d),
        compiler_params=_cparams(("parallel", "arbitrary")),
        name="out_proj",
    )(x, y_ssm, y_attn, w_out[:width].astype(BF16), w_out[width:].astype(BF16),
      mod.reshape(bsz, 1, 3 * d))


def kernel(x, c, rel_bias, norm_g, w_ada, b_ada, w_in, q_gain, k_gain, a_re, a_im, log_dt,
           b_re, b_im, c_re, c_im, d_skip, w_glu, b_glu, w_out):
    bsz, seq, d = x.shape
    depth = w_in.shape[0]
    width = d // 2
    n_heads = width // HEAD_DIM
    topk = min(TOPK_MAX, seq // 4)
    assert seq % min(ATT_QUERIES, seq) == 0 and ATT_QUERIES % ATT_TILE == 0
    assert seq % ATT_TILE == 0 and seq % SSM_CHUNK == 0
    assert ATT_TILE + 1 >= MAX_DISTANCE
    tz = _bias_prep(rel_bias, n_heads)
    for l in range(depth):
        mod = _adaln_mod(c, w_ada[l], b_ada[l])
        u, zs, k, kidx, qt, vt, qit, azt, wt = _in_proj(x, mod, norm_g[l], w_in[l],
                                                        q_gain[l], k_gain[l])
        tabs = _ssm_prep(a_re[l], a_im[l], log_dt[l], b_re[l], b_im[l])
        y_ssm = _s5_mixer(u, zs, tabs, c_re[l], c_im[l], d_skip[l], w_glu[l], b_glu[l])
        y_attn = _dsa_mixer(k, vt, kidx, qt, qit, azt, wt, tz, topk)
        x = _out_proj(x, y_ssm, y_attn, w_out[l], mod)
    return x
```

```python
import functools
import math

import jax
import jax.numpy as jnp
import numpy as np
from jax import lax
from jax.experimental import pallas as pl
from jax.experimental.pallas import tpu as pltpu

F32 = jnp.float32
BF16 = jnp.bfloat16

SSM_GROUP = 16
SSM_STATE = 64
HEAD_DIM = 64
VAUG = 16
IDX_HEADS = 8
IDX_DIM = 64
TOPK_MAX = 256
NUM_BUCKETS = 32
MAX_DISTANCE = 128
EPS = 1e-6

LANES = 128
V7X_VMEM_LIMIT_BYTES = 56 * 1024 * 1024

SSM_CHUNK = 128
SSM_TILE = 256
SSM_BLOCK_CH = 128
PROJ_TILE = 512
ATT_TILE = 256
ATT_QUERIES = 512
BISECT_MIN_ITERS = 14
BISECT_MAX_ITERS = 26
NEG = -1e30
LOG2E = math.log2(math.e)

NT_DIMS = (((1,), (1,)), ((), ()))


def _cparams(sem):
    return pltpu.CompilerParams(dimension_semantics=sem,
                                vmem_limit_bytes=V7X_VMEM_LIMIT_BYTES)


def _sigmoid(x):
    return 1.0 / (1.0 + jnp.exp(-x))


def _silu(x):
    return x * _sigmoid(x)


def _ssm_prep_kernel(are_r, aim_r, ldt_r, are_c, aim_c, ldt_c, bre, bim,
                     ep_re, ep_im, em_re, em_im, laml_re, laml_im, bb_re, bb_im):
    dt = jnp.exp(ldt_r[...])
    ar = are_r[...] * dt
    ai = aim_r[...] * dt
    n = ar.shape[-1]
    tau = lax.broadcasted_iota(jnp.int32, (SSM_CHUNK, n), 0).astype(F32)
    mag_p = jnp.exp(tau * ar)
    mag_m = jnp.exp(-tau * ar)
    cs = jnp.cos(tau * ai)
    sn = jnp.sin(tau * ai)
    ep_re[...] = mag_p * cs
    ep_im[...] = mag_p * sn
    em_re[...] = mag_m * cs
    em_im[...] = -(mag_m * sn)
    mag_l = jnp.exp(float(SSM_CHUNK) * ar)
    laml_re[...] = mag_l * jnp.cos(float(SSM_CHUNK) * ai)
    laml_im[...] = mag_l * jnp.sin(float(SSM_CHUNK) * ai)
    a = are_c[...]
    b = aim_c[...]
    dtc = jnp.exp(ldt_c[...])
    mag = jnp.exp(a * dtc)
    x = mag * jnp.cos(b * dtc) - 1.0
    y = mag * jnp.sin(b * dtc)
    den = a * a + b * b
    cre = (x * a + y * b) / den
    cim = (y * a - x * b) / den
    bb_re[...] = cre * bre[...] - cim * bim[...]
    bb_im[...] = cre * bim[...] + cim * bre[...]


def _ssm_prep(a_re, a_im, log_dt, b_re, b_im):
    g, p = a_re.shape
    n = g * p
    ldt = jnp.broadcast_to(log_dt[:, None], (g, p))
    row = lambda t: t.reshape(1, n)
    col = lambda t: t.reshape(n, 1)
    tab = jax.ShapeDtypeStruct((SSM_CHUNK, n), F32)
    vec = jax.ShapeDtypeStruct((1, n), F32)
    bsh = jax.ShapeDtypeStruct((n, SSM_GROUP), F32)
    return pl.pallas_call(
        _ssm_prep_kernel,
        out_shape=(tab, tab, tab, tab, vec, vec, bsh, bsh),
        name="ssm_prep",
    )(row(a_re), row(a_im), row(ldt), col(a_re), col(a_im), col(ldt),
      b_re.reshape(n, SSM_GROUP), b_im.reshape(n, SSM_GROUP))


def _bias_prep_kernel(rb_ref, o_ref):
    t = o_ref.shape[-1]
    n_heads = o_ref.shape[1]
    i = lax.broadcasted_iota(jnp.int32, (t, t), 0)
    j = lax.broadcasted_iota(jnp.int32, (t, t), 1)
    max_exact = NUM_BUCKETS // 2

    def bias(idx, h):
        dist = jnp.maximum(idx * t + j - i, 0)
        d = jnp.maximum(dist, 1).astype(F32)
        large = max_exact + (jnp.log(d / max_exact) / math.log(MAX_DISTANCE / max_exact)
                             * (NUM_BUCKETS - max_exact)).astype(jnp.int32)
        large = jnp.minimum(large, NUM_BUCKETS - 1)
        bucket = jnp.where(dist < max_exact, dist, large)
        acc = jnp.zeros((t, t), F32)
        for b in range(NUM_BUCKETS):
            acc = jnp.where(bucket == b, rb_ref[b, h], acc)
        return acc

    for h in range(n_heads):
        far = bias(2, h)
        for idx in range(3):
            o_ref[idx, h] = (bias(idx, h) - far) * LOG2E


def _bias_prep(rel_bias, n_heads):
    t = ATT_TILE
    return pl.pallas_call(
        _bias_prep_kernel,
        out_shape=jax.ShapeDtypeStruct((3, n_heads, t, t), F32),
        in_specs=[pl.BlockSpec(memory_space=pltpu.SMEM)],
        name="bias_prep",
    )(rel_bias)


def _mod_kernel(c_ref, w_ref, b_ref, o_ref):
    c = c_ref[...]
    cond = _silu(c)
    o_ref[...] = jnp.dot(cond.astype(BF16), w_ref[...].astype(BF16),
                         preferred_element_type=F32) + b_ref[...]


def _adaln_mod(c, w_ada, b_ada):
    bsz, d = c.shape
    n = w_ada.shape[1]
    tn = 512
    return pl.pallas_call(
        _mod_kernel,
        out_shape=jax.ShapeDtypeStruct((bsz, n), F32),
        grid=(n // tn,),
        in_specs=[pl.BlockSpec((bsz, d), lambda i: (0, 0)),
                  pl.BlockSpec((d, tn), lambda i: (0, i)),
                  pl.BlockSpec((1, tn), lambda i: (0, i))],
        out_specs=pl.BlockSpec((bsz, tn), lambda i: (0, i)),
        compiler_params=_cparams(("arbitrary",)),
        name="adaln_mod",
    )(c, w_ada, b_ada.reshape(1, n))


def _in_proj_kernel(x_ref, mod_ref, g_ref, wnat_ref, wt_ref, ww_ref, mblk_ref,
                    kg_ref, qg_ref,
                    u_ref, zs_ref, k_ref, kidx_ref, qt_ref, vt_ref, qit_ref, azt_ref, wt_out_ref,
                    *, d_model, width, n_heads):
    x = x_ref[0]
    shift = mod_ref[0, :, 0:d_model]
    scale = mod_ref[0, :, d_model:2 * d_model]
    ms = jnp.mean(x * x, axis=-1, keepdims=True)
    xn = x * lax.rsqrt(ms + EPS) * g_ref[...]
    h = (xn * (1.0 + scale) + shift).astype(BF16)
    tm = h.shape[0]

    nat = jnp.dot(h, wnat_ref[...], preferred_element_type=F32)
    u_ref[0] = nat[:, 0:width].astype(BF16)
    zs_ref[0] = nat[:, width:2 * width].astype(BF16)
    k = nat[:, 2 * width:3 * width]
    kms = jnp.dot((k * k).astype(BF16), mblk_ref[...], preferred_element_type=F32)
    k_ref[0] = (k * lax.rsqrt(kms + EPS) * kg_ref[...]).astype(BF16)
    kidx_ref[0] = nat[:, 3 * width:3 * width + LANES].astype(BF16)

    tr = lax.dot_general(wt_ref[...], h, NT_DIMS, preferred_element_type=F32)
    q3 = tr[0:width].reshape(n_heads, HEAD_DIM, tm)
    qms = jnp.mean(q3 * q3, axis=1, keepdims=True)
    qn = q3 * lax.rsqrt(qms + EPS) * qg_ref[...][None] * (HEAD_DIM ** -0.5 * LOG2E)
    qt_ref[0] = qn.reshape(width, tm).astype(BF16)
    v3 = tr[width:2 * width].astype(BF16).reshape(n_heads, HEAD_DIM, tm)
    aug = lax.broadcasted_iota(jnp.int32, (n_heads, VAUG, tm), 1)
    aug = jnp.where(aug == 0, 1.0, 0.0).astype(BF16)
    vt_ref[0] = jnp.concatenate([v3, aug], axis=1).reshape(n_heads * (HEAD_DIM + VAUG), tm)
    qit_ref[0] = tr[2 * width:3 * width].astype(BF16)
    azt_ref[0] = tr[3 * width:4 * width].astype(BF16)
    wt_out_ref[0] = lax.dot_general(ww_ref[...], h, NT_DIMS,
                                    preferred_element_type=F32) * (IDX_HEADS ** -0.5)


def _in_proj(x, mod, norm_g, w_in, q_gain, k_gain):
    bsz, seq, d = x.shape
    width = d // 2
    n_heads = width // HEAD_DIM
    tm = min(PROJ_TILE, seq)
    o = np.cumsum([0, width, width, width, width, width, width,
                   IDX_HEADS * IDX_DIM, IDX_DIM, IDX_HEADS])
    col = lambda i: w_in[:, int(o[i]):int(o[i + 1])]
    w_kidx = jnp.pad(col(7), ((0, 0), (0, LANES - IDX_DIM)))
    w_nat = jnp.concatenate([col(0), col(1), col(3), w_kidx], axis=1).astype(BF16)
    w_t = jnp.concatenate([col(2), col(4), col(6), col(5)], axis=1).T.astype(BF16)
    w_w = col(8).T.astype(BF16)
    hid = np.arange(width) // HEAD_DIM
    mblk = jnp.asarray((hid[:, None] == hid[None, :]).astype(np.float32) / HEAD_DIM, BF16)
    kg = jnp.tile(k_gain, n_heads).reshape(1, width)
    qg = q_gain.reshape(HEAD_DIM, 1)

    nat_w = w_nat.shape[1]
    bs = jax.ShapeDtypeStruct
    const = lambda shape: pl.BlockSpec(shape, lambda b, s: (0,) * len(shape))
    tok = lambda w: pl.BlockSpec((1, tm, w), lambda b, s: (b, s, 0))
    trn = lambda r: pl.BlockSpec((1, r, tm), lambda b, s: (b, 0, s))
    kern = functools.partial(_in_proj_kernel, d_model=d, width=width, n_heads=n_heads)
    vrows = n_heads * (HEAD_DIM + VAUG)
    return pl.pallas_call(
        kern,
        out_shape=(bs((bsz, seq, width), BF16), bs((bsz, seq, width), BF16),
                   bs((bsz, seq, width), BF16), bs((bsz, seq, LANES), BF16),
                   bs((bsz, width, seq), BF16), bs((bsz, vrows, seq), BF16),
                   bs((bsz, width, seq), BF16), bs((bsz, width, seq), BF16),
                   bs((bsz, IDX_HEADS, seq), F32)),
        grid=(bsz, seq // tm),
        in_specs=[tok(d),
                  pl.BlockSpec((1, 1, 3 * d), lambda b, s: (b, 0, 0)),
                  const((1, d)), const((d, nat_w)), const((4 * width, d)),
                  const((IDX_HEADS, d)), const((width, width)),
                  const((1, width)), const((HEAD_DIM, 1))],
        out_specs=(tok(width), tok(width), tok(width), tok(LANES),
                   trn(width), trn(vrows), trn(width), trn(width), trn(IDX_HEADS)),
        compiler_params=_cparams(("parallel", "arbitrary")),
        name="in_proj",
    )(x, mod.reshape(bsz, 1, 3 * d), norm_g.reshape(1, d), w_nat, w_t, w_w, mblk, kg, qg)


def _s5_kernel(u_ref, zs_ref, bblk_ref, cre_ref, cim_ref, ep_re, ep_im, em_re, em_im,
               laml_re, laml_im, tri_ref, dsk_ref, wglu_ref, bglu_ref,
               xres_ref, ya_ref, w1_ref, w2_ref, mod_ref,
               o_ref, car_re, car_im, w_ref, x_ref, *, n_blocks, d_model):
    tm = u_ref.shape[1]
    nch = tm // SSM_CHUNK
    nst = cre_ref.shape[1]

    @pl.when(pl.program_id(1) == 0)
    def _():
        car_re[...] = jnp.zeros_like(car_re)
        car_im[...] = jnp.zeros_like(car_im)

    tri = tri_ref[...]
    for j in range(n_blocks):
        st = slice(j * nst, (j + 1) * nst)
        ch = slice(j * SSM_BLOCK_CH, (j + 1) * SSM_BLOCK_CH)
        p = jnp.dot(u_ref[0, :, ch], bblk_ref[j], preferred_element_type=F32)
        p_re = p[:, :nst].astype(BF16).reshape(nch, SSM_CHUNK, nst)
        p_im = p[:, nst:].astype(BF16).reshape(nch, SSM_CHUNK, nst)
        emr = em_re[:, st][None]
        emi = em_im[:, st][None]
        w_ref[j, :, 0:nst] = (p_re * emr - p_im * emi).reshape(tm, nst)
        w_ref[j, :, nst:2 * nst] = (p_re * emi + p_im * emr).reshape(tm, nst)
    for j in range(n_blocks):
        st = slice(j * nst, (j + 1) * nst)
        cum = jnp.dot(tri, w_ref[j], preferred_element_type=F32)
        cum_re = cum[:, :nst].reshape(nch, SSM_CHUNK, nst)
        cum_im = cum[:, nst:].reshape(nch, SSM_CHUNK, nst)
        lr = laml_re[:, st]
        li = laml_im[:, st]
        cr = car_re[:, st]
        ci = car_im[:, st]
        crs, cis = [], []
        for c in range(nch):
            crs.append(cr)
            cis.append(ci)
            tr_ = cum_re[c, SSM_CHUNK - 1:SSM_CHUNK, :] + cr
            ti_ = cum_im[c, SSM_CHUNK - 1:SSM_CHUNK, :] + ci
            cr = lr * tr_ - li * ti_
            ci = lr * ti_ + li * tr_
        car_re[:, st] = cr
        car_im[:, st] = ci
        a_re = (cum_re + jnp.stack(crs, axis=0)).astype(BF16)
        a_im = (cum_im + jnp.stack(cis, axis=0)).astype(BF16)
        epr = ep_re[:, st][None]
        epi = ep_im[:, st][None]
        x_ref[j, :, 0:nst] = (a_re * epr - a_im * epi).reshape(tm, nst)
        x_ref[j, :, nst:2 * nst] = (a_re * epi + a_im * epr).reshape(tm, nst)
    ys = []
    for j in range(n_blocks):
        ys.append(jnp.dot(x_ref[j, :, 0:nst], cre_ref[j], preferred_element_type=F32)
                  - jnp.dot(x_ref[j, :, nst:2 * nst], cim_ref[j], preferred_element_type=F32))
    y = jnp.concatenate(ys, axis=1) + dsk_ref[...] * u_ref[0].astype(F32)
    z = 0.5 * y * (1.0 + jnp.tanh(math.sqrt(2.0 / math.pi) * (y + 0.044715 * (y * y * y))))
    gl = jnp.dot(z.astype(BF16), wglu_ref[...], preferred_element_type=F32) + bglu_ref[...]
    zz = z * _sigmoid(gl)
    y_ssm = (zz * _silu(zs_ref[0].astype(F32))).astype(BF16)
    gate = mod_ref[0, :, 2 * d_model:3 * d_model]
    yo = (jnp.dot(y_ssm, w1_ref[...], preferred_element_type=F32)
          + jnp.dot(ya_ref[0], w2_ref[...], preferred_element_type=F32))
    o_ref[0] = xres_ref[0] + gate * yo


def _s5_mixer_out(u, zs, tabs, c_re, c_im, d_skip, w_glu, b_glu, x, y_attn, w_out, mod):
    bsz, seq, width = u.shape
    d = x.shape[-1]
    ep_re, ep_im, em_re, em_im, laml_re, laml_im, bb_re, bb_im = tabs
    tm = min(SSM_TILE, seq)
    gpb = SSM_BLOCK_CH // SSM_GROUP
    n_blocks = width // SSM_BLOCK_CH
    nst = gpb * SSM_STATE
    eye = jnp.eye(gpb, dtype=F32)

    def b_block(t):
        t4 = t.reshape(n_blocks, gpb, SSM_STATE, SSM_GROUP)
        return jnp.einsum('jgpc,gh->jgchp', t4, eye).reshape(n_blocks, SSM_BLOCK_CH, nst)

    def c_block(t):
        t4 = t.reshape(n_blocks, gpb, SSM_GROUP, SSM_STATE)
        return jnp.einsum('jgcp,gh->jhpgc', t4, eye).reshape(n_blocks, nst, SSM_BLOCK_CH)

    bblk = jnp.concatenate([b_block(bb_re), b_block(bb_im)], axis=2).astype(BF16)
    cre = c_block(c_re).astype(BF16)
    cim = c_block(c_im).astype(BF16)
    t_idx = np.arange(tm)
    tri = jnp.asarray(((t_idx[:, None] >= t_idx[None, :])
                       & (t_idx[:, None] // SSM_CHUNK == t_idx[None, :] // SSM_CHUNK)
                       ).astype(np.float32), BF16)
    n = ep_re.shape[1]
    const = lambda shape: pl.BlockSpec(shape, lambda b, s: (0,) * len(shape))
    tok = pl.BlockSpec((1, tm, width), lambda b, s: (b, s, 0))
    tokd = pl.BlockSpec((1, tm, d), lambda b, s: (b, s, 0))
    kern = functools.partial(_s5_kernel, n_blocks=n_blocks, d_model=d)
    return pl.pallas_call(
        kern,
        out_shape=jax.ShapeDtypeStruct((bsz, seq, d), x.dtype),
        grid=(bsz, seq // tm),
        in_specs=[tok, tok,
                  const((n_blocks, SSM_BLOCK_CH, 2 * nst)),
                  const((n_blocks, nst, SSM_BLOCK_CH)), const((n_blocks, nst, SSM_BLOCK_CH)),
                  const((SSM_CHUNK, n)), const((SSM_CHUNK, n)),
                  const((SSM_CHUNK, n)), const((SSM_CHUNK, n)),
                  const((1, n)), const((1, n)), const((tm, tm)),
                  const((1, width)), const((width, width)), const((1, width)),
                  tokd, tok, const((width, d)), const((width, d)),
                  pl.BlockSpec((1, 1, 3 * d), lambda b, s: (b, 0, 0))],
        out_specs=tokd,
        scratch_shapes=[pltpu.VMEM((1, n), F32), pltpu.VMEM((1, n), F32),
                        pltpu.VMEM((n_blocks, tm, 2 * nst), BF16),
                        pltpu.VMEM((n_blocks, tm, 2 * nst), BF16)],
        compiler_params=_cparams(("parallel", "arbitrary")),
        name="s5_mixer_out",
    )(u, zs, bblk, cre, cim, ep_re.astype(BF16), ep_im.astype(BF16),
      em_re.astype(BF16), em_im.astype(BF16), laml_re, laml_im, tri,
      d_skip.reshape(1, width), w_glu.astype(BF16), b_glu.reshape(1, width),
      x, y_attn, w_out[:width].astype(BF16), w_out[width:].astype(BF16),
      mod.reshape(bsz, 1, 3 * d))


def _dsa_kernel(k_ref, vt_ref, kidx_ref, qt_ref, qit_ref, azt_ref, wt_ref, tz_ref,
                o_ref, sc_ref, qip_ref, qp_ref, m_ref, acc_ref, s_ref, s2_ref, mt_ref, mt2_ref,
                p_ref, p2_ref, c_ref, c2_ref, sel_ref,
                *, topk, n_heads):
    t = ATT_TILE
    tq = qt_ref.shape[2]
    kpq = tq // t
    qi = pl.program_id(1)
    n_far = qi * kpq
    nk = n_far + kpq
    seq = k_ref.shape[1]
    row = lax.broadcasted_iota(jnp.int32, (t, tq), 0)
    colq = lax.broadcasted_iota(jnp.int32, (t, tq), 1)
    lane = lax.broadcasted_iota(jnp.int32, (1, tq), 1)
    pos = qi * tq + lane
    kq = jnp.minimum(pos + 1, topk).astype(F32)

    def tile_rows(kt):
        return pl.ds(pl.multiple_of(kt * t, t), t)

    zpad = jnp.zeros((HEAD_DIM, tq), BF16)
    for h in range(IDX_HEADS):
        qip_ref[h] = jnp.concatenate([qit_ref[0, h * IDX_DIM:(h + 1) * IDX_DIM, :], zpad], axis=0)
    for h in range(n_heads):
        qh = qt_ref[0, h * HEAD_DIM:(h + 1) * HEAD_DIM, :]
        qp_ref[h] = jnp.concatenate([qh, zpad] if h % 2 == 0 else [zpad, qh], axis=0)

    def score_tile(kt):
        kx = kidx_ref[0, tile_rows(kt), :]
        s = jnp.zeros((t, tq), F32)
        for h in range(IDX_HEADS):
            rel = jnp.dot(kx, qip_ref[h], preferred_element_type=F32)
            s = s + wt_ref[0, h:h + 1, :] * jnp.maximum(rel, 0.0)
        return s

    def score_far(kt, carry):
        smin, smax = carry
        s = score_tile(kt)
        sc_ref[tile_rows(kt), :] = s
        return (jnp.minimum(smin, jnp.min(s, axis=0, keepdims=True)),
                jnp.maximum(smax, jnp.max(s, axis=0, keepdims=True)))

    smin, smax = lax.fori_loop(
        0, n_far, score_far,
        (jnp.full((1, tq), jnp.inf, F32), jnp.full((1, tq), -jnp.inf, F32)))
    for d in range(kpq):
        s = score_tile(n_far + d)
        causal = (row + d * t) <= colq
        sc_ref[tile_rows(n_far + d), :] = jnp.where(causal, s, -jnp.inf)
        smin = jnp.minimum(smin, jnp.min(jnp.where(causal, s, jnp.inf), axis=0, keepdims=True))
        smax = jnp.maximum(smax, jnp.max(jnp.where(causal, s, -jnp.inf), axis=0, keepdims=True))

    def count_where(pred_fn):
        def body(kt, acc):
            m = pred_fn(sc_ref[tile_rows(kt), :])
            return acc + jnp.sum(jnp.where(m, 1.0, 0.0).reshape(4, t // 32, 8, tq), axis=1)
        acc = lax.fori_loop(0, nk, body, jnp.zeros((4, 8, tq), F32))
        return jnp.sum(acc.reshape(32, tq), axis=0, keepdims=True)

    def count_ge(thr):
        return count_where(lambda s_: s_ >= thr)

    def count_gt(thr):
        return count_where(lambda s_: s_ > thr)

    def min_where(pred_fn):
        def body(kt, acc):
            s = sc_ref[tile_rows(kt), :]
            return jnp.minimum(acc, jnp.min(jnp.where(pred_fn(s), s, jnp.inf), axis=0, keepdims=True))
        return lax.fori_loop(0, nk, body, jnp.full((1, tq), jnp.inf, F32))

    def bisect(st):
        lo, hi, c_lo = st
        mid = lo + 0.5 * (hi - lo)
        cnt = count_ge(mid)
        up = cnt >= kq
        return jnp.where(up, mid, lo), jnp.where(up, hi, mid), jnp.where(up, cnt, c_lo)

    c_ge0 = count_ge(0.0)
    c_gt0 = count_gt(0.0)
    above = c_gt0 >= kq
    below = c_ge0 < kq
    at_zero = jnp.logical_not(above | below)
    zero_tie = at_zero & (c_ge0 > kq)
    lo_neg = above & (smin <= 0.0)
    lo0 = jnp.where(at_zero | lo_neg, 0.0, smin)
    c_lo0 = jnp.where(at_zero | lo_neg, c_ge0, (pos + 1).astype(F32))
    hi0 = jnp.where(at_zero, 0.0, jnp.where(below, jnp.minimum(smax, 0.0), smax))
    open_f = jnp.where(zero_tie, 0.0, 1.0)

    st = lax.fori_loop(0, BISECT_MIN_ITERS, lambda _, s_: bisect(s_), (lo0, hi0, c_lo0))

    def more_cond(c):
        return (c[3] > 0.0) & (c[4] < BISECT_MAX_ITERS)

    def more_body(c):
        st_ = (c[0], c[1], c[2])
        flag = jnp.max((c[2] - kq) * open_f)
        return bisect(st_) + (flag, c[4] + 1)

    lo, _, c_lo, _, _ = lax.while_loop(
        more_cond, more_body, st + (jnp.float32(1.0), jnp.int32(BISECT_MIN_ITERS)))

    sel_ref[0:1, :] = lo
    sel_ref[1:2, :] = jnp.zeros((1, tq), F32)
    sel_ref[2:3, :] = kq - c_gt0
    sel_ref[3:4, :] = jnp.where(zero_tie, 1.0, 0.0)

    @pl.when(jnp.max((c_lo - kq) * open_f) > 0.0)
    def _():
        def peel_cond(c):
            return c[6] > 0.0

        def peel(c):
            lo_, c_lo_, taup_, ngt_, gtie_ = c[0], c[1], c[2], c[3], c[4]
            taup_n = min_where(lambda s_: s_ >= lo_)
            ngt_n = count_gt(taup_n)
            active = (c_lo_ > kq) & (open_f > 0.0) & (gtie_ == 0.0)
            is_tie = active & (ngt_n < kq)
            need_peel = active & (ngt_n >= kq)
            nxt = min_where(lambda s_: s_ > taup_n)
            return (jnp.where(need_peel, nxt, lo_), jnp.where(need_peel, ngt_n, c_lo_),
                    jnp.where(is_tie, taup_n, taup_), jnp.where(is_tie, ngt_n, ngt_),
                    jnp.where(is_tie, 1.0, gtie_), c[5],
                    jnp.max(jnp.where(need_peel, 1.0, 0.0)))

        zero = jnp.zeros((1, tq), F32)
        lo2, _, taup2, ngt2, gtie2, _, _ = lax.while_loop(
            peel_cond, peel, (lo, c_lo, zero, zero, zero, zero, jnp.float32(1.0)))
        gt = gtie2 > 0.0
        sel_ref[0:1, :] = lo2
        sel_ref[1:2, :] = jnp.where(gt, taup2, sel_ref[1:2, :])
        sel_ref[2:3, :] = jnp.where(gt, kq - ngt2, sel_ref[2:3, :])
        sel_ref[3:4, :] = jnp.where(gt, 1.0, sel_ref[3:4, :])

    lo_f = sel_ref[0:1, :]
    tie_f = sel_ref[3:4, :]
    any_tie = jnp.max(tie_f)

    @pl.when(any_tie <= 0.0)
    def _():
        def body(kt, c):
            s_ = sc_ref[tile_rows(kt), :]
            sc_ref[tile_rows(kt), :] = jnp.where(s_ >= lo_f, 0.0, NEG)
            return c
        lax.fori_loop(0, nk, body, 0)

    @pl.when(any_tie > 0.0)
    def _():
        taup = sel_ref[1:2, :]
        need = jnp.where(tie_f > 0.0, sel_ref[2:3, :], float(2 * seq))
        tr_ = lax.broadcasted_iota(jnp.int32, (t, t), 0)
        tc_ = lax.broadcasted_iota(jnp.int32, (t, t), 1)
        tri = jnp.where(tc_ <= tr_, 1.0, 0.0).astype(BF16)

        def body(kt, seen):
            s_ = sc_ref[tile_rows(kt), :]
            z = s_ == taup
            pref = jnp.dot(tri, jnp.where(z, 1.0, 0.0).astype(BF16), preferred_element_type=F32)
            keep = (s_ >= lo_f) & jnp.logical_not(z & (seen + pref > need))
            sc_ref[tile_rows(kt), :] = jnp.where(keep, 0.0, NEG)
            return seen + pref[t - 1:t, :]
        lax.fori_loop(0, nk, body, jnp.zeros((1, tq), F32))

    va = HEAD_DIM + VAUG

    def logits_head(h, kt, s_buf, mt_buf):
        rows = tile_rows(kt)
        pair = slice((h // 2) * 2 * HEAD_DIM, (h // 2 + 1) * 2 * HEAD_DIM)
        s = jnp.dot(k_ref[0, rows, pair], qp_ref[h], preferred_element_type=F32)
        s = s + sc_ref[rows, :]
        parts = []
        for g in range(tq // t):
            near = jnp.clip(n_far + g - kt, 0, 2)
            parts.append(s[:, g * t:(g + 1) * t] + tz_ref[near, h])
        s = jnp.concatenate(parts, axis=1) if len(parts) > 1 else parts[0]
        s_buf[h] = s
        mt_buf[h:h + 1, :] = jnp.max(s, axis=0, keepdims=True)

    def softmax_head(h, s_buf, mt_buf, p_buf, corr_buf):
        m_old = m_ref[h:h + 1, :]
        m_new = jnp.maximum(m_old, mt_buf[h:h + 1, :])
        corr_buf[h:h + 1, :] = jnp.exp2(m_old - m_new)
        p_buf[h] = jnp.exp2(s_buf[h] - m_new).astype(BF16)
        m_ref[h:h + 1, :] = m_new

    def pv_head(h, kt, p_buf, corr_buf):
        rows = tile_rows(kt)
        hs = slice(h * va, (h + 1) * va)
        acc_ref[hs, :] = acc_ref[hs, :] * corr_buf[h:h + 1, :] + jnp.dot(
            vt_ref[0, hs, rows], p_buf[h], preferred_element_type=F32)

    def logits_stage(kt, s_buf, mt_buf):
        for h in range(n_heads):
            logits_head(h, kt, s_buf, mt_buf)

    def pv_stage(kt, p_buf, corr_buf):
        for h in range(n_heads):
            pv_head(h, kt, p_buf, corr_buf)

    def trip(kt_pv, p_old, c_old, kt_next, s_next, mt_next, s_cur, mt_cur, p_cur, c_cur):
        for h in range(n_heads):
            pv_head(h, kt_pv, p_old, c_old)
            logits_head(h, kt_next, s_next, mt_next)
            softmax_head(h, s_cur, mt_cur, p_cur, c_cur)

    m_ref[...] = jnp.full_like(m_ref, NEG)
    acc_ref[...] = jnp.zeros_like(acc_ref)
    p2_ref[...] = jnp.zeros_like(p2_ref)
    c2_ref[...] = jnp.ones_like(c2_ref)
    logits_stage(0, s_ref, mt_ref)
    last = nk - 1

    def kv_body(j, c):
        i = 2 * j
        trip(jnp.maximum(i - 1, 0), p2_ref, c2_ref,
             jnp.minimum(i + 1, last), s2_ref, mt2_ref,
             s_ref, mt_ref, p_ref, c_ref)

        @pl.when(i + 1 < nk)
        def _():
            trip(i, p_ref, c_ref,
                 jnp.minimum(i + 2, last), s_ref, mt_ref,
                 s2_ref, mt2_ref, p2_ref, c2_ref)
        return c

    lax.fori_loop(0, (nk + 1) // 2, kv_body, 0)

    if kpq % 2 == 0:
        pv_stage(last, p2_ref, c2_ref)
    else:
        @pl.when(jnp.bitwise_and(nk, 1) == 1)
        def _():
            pv_stage(last, p_ref, c_ref)

        @pl.when(jnp.bitwise_and(nk, 1) == 0)
        def _():
            pv_stage(last, p2_ref, c2_ref)

    acc = acc_ref[...].reshape(n_heads, va, tq)
    out = acc[:, :HEAD_DIM, :] * (1.0 / acc[:, HEAD_DIM:HEAD_DIM + 1, :])
    out = out.reshape(n_heads * HEAD_DIM, tq) * _silu(azt_ref[0].astype(F32))
    o_ref[0] = out.T.astype(BF16)


def _dsa_mixer(k, vt, kidx, qt, qit, azt, wt, tz, topk):
    bsz, seq, width = k.shape
    n_heads = width // HEAD_DIM
    t = ATT_TILE
    tq = min(ATT_QUERIES, seq)
    per_b = lambda shape: pl.BlockSpec((1,) + shape, lambda b, q: (b, 0, 0))
    qtile = lambda r: pl.BlockSpec((1, r, tq), lambda b, q: (b, 0, q))
    kern = functools.partial(_dsa_kernel, topk=topk, n_heads=n_heads)
    vrows = vt.shape[1]
    return pl.pallas_call(
        kern,
        out_shape=jax.ShapeDtypeStruct((bsz, seq, width), BF16),
        grid=(bsz, seq // tq),
        in_specs=[per_b((seq, width)), per_b((vrows, seq)), per_b((seq, LANES)),
                  qtile(width), qtile(width), qtile(width), qtile(IDX_HEADS),
                  pl.BlockSpec((3, n_heads, t, t), lambda b, q: (0, 0, 0, 0))],
        out_specs=pl.BlockSpec((1, tq, width), lambda b, q: (b, q, 0)),
        scratch_shapes=[pltpu.VMEM((seq, tq), F32),
                        pltpu.VMEM((IDX_HEADS, 2 * IDX_DIM, tq), BF16),
                        pltpu.VMEM((n_heads, 2 * HEAD_DIM, tq), BF16),
                        pltpu.VMEM((n_heads, tq), F32),
                        pltpu.VMEM((vrows, tq), F32),
                        pltpu.VMEM((n_heads, t, tq), F32), pltpu.VMEM((n_heads, t, tq), F32),
                        pltpu.VMEM((n_heads, tq), F32), pltpu.VMEM((n_heads, tq), F32),
                        pltpu.VMEM((n_heads, t, tq), BF16), pltpu.VMEM((n_heads, t, tq), BF16),
                        pltpu.VMEM((n_heads, tq), F32), pltpu.VMEM((n_heads, tq), F32),
                        pltpu.VMEM((8, tq), F32)],
        compiler_params=_cparams(("parallel", "arbitrary")),
        name="dsa_mixer",
    )(k, vt, kidx, qt, qit, azt, wt, tz)


def kernel(x, c, rel_bias, norm_g, w_ada, b_ada, w_in, q_gain, k_gain, a_re, a_im, log_dt,
           b_re, b_im, c_re, c_im, d_skip, w_glu, b_glu, w_out):
    bsz, seq, d = x.shape
    depth = w_in.shape[0]
    width = d // 2
    n_heads = width // HEAD_DIM
    topk = min(TOPK_MAX, seq // 4)
    assert seq % min(ATT_QUERIES, seq) == 0 and ATT_QUERIES % ATT_TILE == 0
    assert seq % ATT_TILE == 0 and seq % SSM_CHUNK == 0
    assert ATT_TILE + 1 >= MAX_DISTANCE
    tz = _bias_prep(rel_bias, n_heads)
    for l in range(depth):
        mod = _adaln_mod(c, w_ada[l], b_ada[l])
        u, zs, k, kidx, qt, vt, qit, azt, wt = _in_proj(x, mod, norm_g[l], w_in[l],
                                                        q_gain[l], k_gain[l])
        tabs = _ssm_prep(a_re[l], a_im[l], log_dt[l], b_re[l], b_im[l])
        y_attn = _dsa_mixer(k, vt, kidx, qt, qit, azt, wt, tz, topk)
        x = _s5_mixer_out(u, zs, tabs, c_re[l], c_im[l], d_skip[l], w_glu[l], b_glu[l],
                          x, y_attn, w_out[l], mod)
    return x
```

```python
import functools
import math

import jax
import jax.numpy as jnp
import numpy as np
from jax import lax
from jax.experimental import pallas as pl
from jax.experimental.pallas import tpu as pltpu

F32 = jnp.float32
BF16 = jnp.bfloat16

SSM_GROUP = 16
SSM_STATE = 64
HEAD_DIM = 64
VAUG = 16
IDX_HEADS = 8
IDX_DIM = 64
TOPK_MAX = 256
NUM_BUCKETS = 32
MAX_DISTANCE = 128
EPS = 1e-6

LANES = 128
V7X_VMEM_LIMIT_BYTES = 56 * 1024 * 1024

SSM_CHUNK = 128
SSM_TILE = 256
SSM_BLOCK_CH = 128
PROJ_TILE = 512
ATT_TILE = 256
ATT_QUERIES = 512
BISECT_MIN_ITERS = 14
BISECT_MAX_ITERS = 26
NEG = -(2.0 ** 100)
LOG2E = math.log2(math.e)

NT_DIMS = (((1,), (1,)), ((), ()))


def _cparams(sem):
    return pltpu.CompilerParams(dimension_semantics=sem,
                                vmem_limit_bytes=V7X_VMEM_LIMIT_BYTES)


def _sigmoid(x):
    return 1.0 / (1.0 + jnp.exp(-x))


def _silu(x):
    return x * _sigmoid(x)


def _ssm_prep_kernel(are_r, aim_r, ldt_r, are_c, aim_c, ldt_c, bre, bim,
                     ep_re, ep_im, em_re, em_im, laml_re, laml_im, bb_re, bb_im):
    dt = jnp.exp(ldt_r[...])
    ar = are_r[...] * dt
    ai = aim_r[...] * dt
    n = ar.shape[-1]
    tau = lax.broadcasted_iota(jnp.int32, (SSM_CHUNK, n), 0).astype(F32)
    mag_p = jnp.exp(tau * ar)
    mag_m = jnp.exp(-tau * ar)
    cs = jnp.cos(tau * ai)
    sn = jnp.sin(tau * ai)
    ep_re[...] = mag_p * cs
    ep_im[...] = mag_p * sn
    em_re[...] = mag_m * cs
    em_im[...] = -(mag_m * sn)
    mag_l = jnp.exp(float(SSM_CHUNK) * ar)
    laml_re[...] = mag_l * jnp.cos(float(SSM_CHUNK) * ai)
    laml_im[...] = mag_l * jnp.sin(float(SSM_CHUNK) * ai)
    a = are_c[...]
    b = aim_c[...]
    dtc = jnp.exp(ldt_c[...])
    mag = jnp.exp(a * dtc)
    x = mag * jnp.cos(b * dtc) - 1.0
    y = mag * jnp.sin(b * dtc)
    den = a * a + b * b
    cre = (x * a + y * b) / den
    cim = (y * a - x * b) / den
    bb_re[...] = cre * bre[...] - cim * bim[...]
    bb_im[...] = cre * bim[...] + cim * bre[...]


def _ssm_prep(a_re, a_im, log_dt, b_re, b_im):
    g, p = a_re.shape
    n = g * p
    ldt = jnp.broadcast_to(log_dt[:, None], (g, p))
    row = lambda t: t.reshape(1, n)
    col = lambda t: t.reshape(n, 1)
    tab = jax.ShapeDtypeStruct((SSM_CHUNK, n), F32)
    vec = jax.ShapeDtypeStruct((1, n), F32)
    bsh = jax.ShapeDtypeStruct((n, SSM_GROUP), F32)
    return pl.pallas_call(
        _ssm_prep_kernel,
        out_shape=(tab, tab, tab, tab, vec, vec, bsh, bsh),
        name="ssm_prep",
    )(row(a_re), row(a_im), row(ldt), col(a_re), col(a_im), col(ldt),
      b_re.reshape(n, SSM_GROUP), b_im.reshape(n, SSM_GROUP))


def _bias_prep_kernel(rb_ref, o_ref):
    t = o_ref.shape[-1]
    n_heads = o_ref.shape[1]
    i = lax.broadcasted_iota(jnp.int32, (t, t), 0)
    j = lax.broadcasted_iota(jnp.int32, (t, t), 1)
    max_exact = NUM_BUCKETS // 2

    def bias(idx, h):
        dist = jnp.maximum(idx * t + j - i, 0)
        d = jnp.maximum(dist, 1).astype(F32)
        large = max_exact + (jnp.log(d / max_exact) / math.log(MAX_DISTANCE / max_exact)
                             * (NUM_BUCKETS - max_exact)).astype(jnp.int32)
        large = jnp.minimum(large, NUM_BUCKETS - 1)
        bucket = jnp.where(dist < max_exact, dist, large)
        acc = jnp.zeros((t, t), F32)
        for b in range(NUM_BUCKETS):
            acc = jnp.where(bucket == b, rb_ref[b, h], acc)
        return acc

    for h in range(n_heads):
        far = bias(2, h)
        for idx in range(3):
            o_ref[idx, h] = ((bias(idx, h) - far) * LOG2E).astype(o_ref.dtype)


def _bias_prep(rel_bias, n_heads):
    t = ATT_TILE
    return pl.pallas_call(
        _bias_prep_kernel,
        out_shape=jax.ShapeDtypeStruct((3, n_heads, t, t), BF16),
        in_specs=[pl.BlockSpec(memory_space=pltpu.SMEM)],
        name="bias_prep",
    )(rel_bias)


def _mod_kernel(c_ref, w_ref, b_ref, o_ref):
    c = c_ref[...]
    cond = _silu(c)
    o_ref[...] = jnp.dot(cond.astype(BF16), w_ref[...].astype(BF16),
                         preferred_element_type=F32) + b_ref[...]


def _adaln_mod(c, w_ada, b_ada):
    bsz, d = c.shape
    n = w_ada.shape[1]
    tn = 512
    return pl.pallas_call(
        _mod_kernel,
        out_shape=jax.ShapeDtypeStruct((bsz, n), F32),
        grid=(n // tn,),
        in_specs=[pl.BlockSpec((bsz, d), lambda i: (0, 0)),
                  pl.BlockSpec((d, tn), lambda i: (0, i)),
                  pl.BlockSpec((1, tn), lambda i: (0, i))],
        out_specs=pl.BlockSpec((bsz, tn), lambda i: (0, i)),
        compiler_params=_cparams(("arbitrary",)),
        name="adaln_mod",
    )(c, w_ada, b_ada.reshape(1, n))


def _in_proj_kernel(x_ref, mod_ref, g_ref, wnat_ref, wt_ref, ww_ref, mblk_ref,
                    kg_ref, qg_ref,
                    u_ref, zs_ref, k_ref, kidx_ref, qt_ref, vt_ref, qit_ref, azt_ref, wt_out_ref,
                    *, d_model, width, n_heads):
    x = x_ref[0]
    shift = mod_ref[0, :, 0:d_model]
    scale = mod_ref[0, :, d_model:2 * d_model]
    ms = jnp.mean(x * x, axis=-1, keepdims=True)
    xn = x * lax.rsqrt(ms + EPS) * g_ref[...]
    h = (xn * (1.0 + scale) + shift).astype(BF16)
    tm = h.shape[0]

    nat = jnp.dot(h, wnat_ref[...], preferred_element_type=F32)
    u_ref[0] = nat[:, 0:width].astype(BF16)
    zs_ref[0] = nat[:, width:2 * width].astype(BF16)
    k = nat[:, 2 * width:3 * width]
    kms = jnp.dot((k * k).astype(BF16), mblk_ref[...], preferred_element_type=F32)
    k_ref[0] = (k * lax.rsqrt(kms + EPS) * kg_ref[...]).astype(BF16)
    kidx_ref[0] = nat[:, 3 * width:3 * width + LANES].astype(BF16)

    tr = lax.dot_general(wt_ref[...], h, NT_DIMS, preferred_element_type=F32)
    q3 = tr[0:width].reshape(n_heads, HEAD_DIM, tm)
    qms = jnp.mean(q3 * q3, axis=1, keepdims=True)
    qn = q3 * lax.rsqrt(qms + EPS) * qg_ref[...][None] * (HEAD_DIM ** -0.5 * LOG2E)
    qt_ref[0] = qn.reshape(width, tm).astype(BF16)
    v3 = tr[width:2 * width].astype(BF16).reshape(n_heads, HEAD_DIM, tm)
    aug = lax.broadcasted_iota(jnp.int32, (n_heads, VAUG, tm), 1)
    aug = jnp.where(aug == 0, 1.0, 0.0).astype(BF16)
    vt_ref[0] = jnp.concatenate([v3, aug], axis=1).reshape(n_heads * (HEAD_DIM + VAUG), tm)
    qit_ref[0] = tr[2 * width:3 * width].astype(BF16)
    azt_ref[0] = tr[3 * width:4 * width].astype(BF16)
    wt_out_ref[0] = lax.dot_general(ww_ref[...], h, NT_DIMS,
                                    preferred_element_type=F32) * (IDX_HEADS ** -0.5)


def _in_proj(x, mod, norm_g, w_in, q_gain, k_gain):
    bsz, seq, d = x.shape
    width = d // 2
    n_heads = width // HEAD_DIM
    tm = min(PROJ_TILE, seq)
    o = np.cumsum([0, width, width, width, width, width, width,
                   IDX_HEADS * IDX_DIM, IDX_DIM, IDX_HEADS])
    col = lambda i: w_in[:, int(o[i]):int(o[i + 1])]
    w_kidx = jnp.pad(col(7), ((0, 0), (0, LANES - IDX_DIM)))
    w_nat = jnp.concatenate([col(0), col(1), col(3), w_kidx], axis=1).astype(BF16)
    w_t = jnp.concatenate([col(2), col(4), col(6), col(5)], axis=1).T.astype(BF16)
    w_w = col(8).T.astype(BF16)
    hid = np.arange(width) // HEAD_DIM
    mblk = jnp.asarray((hid[:, None] == hid[None, :]).astype(np.float32) / HEAD_DIM, BF16)
    kg = jnp.tile(k_gain, n_heads).reshape(1, width)
    qg = q_gain.reshape(HEAD_DIM, 1)

    nat_w = w_nat.shape[1]
    bs = jax.ShapeDtypeStruct
    const = lambda shape: pl.BlockSpec(shape, lambda b, s: (0,) * len(shape))
    tok = lambda w: pl.BlockSpec((1, tm, w), lambda b, s: (b, s, 0))
    trn = lambda r: pl.BlockSpec((1, r, tm), lambda b, s: (b, 0, s))
    kern = functools.partial(_in_proj_kernel, d_model=d, width=width, n_heads=n_heads)
    vrows = n_heads * (HEAD_DIM + VAUG)
    return pl.pallas_call(
        kern,
        out_shape=(bs((bsz, seq, width), BF16), bs((bsz, seq, width), BF16),
                   bs((bsz, seq, width), BF16), bs((bsz, seq, LANES), BF16),
                   bs((bsz, width, seq), BF16), bs((bsz, vrows, seq), BF16),
                   bs((bsz, width, seq), BF16), bs((bsz, width, seq), BF16),
                   bs((bsz, IDX_HEADS, seq), F32)),
        grid=(bsz, seq // tm),
        in_specs=[tok(d),
                  pl.BlockSpec((1, 1, 3 * d), lambda b, s: (b, 0, 0)),
                  const((1, d)), const((d, nat_w)), const((4 * width, d)),
                  const((IDX_HEADS, d)), const((width, width)),
                  const((1, width)), const((HEAD_DIM, 1))],
        out_specs=(tok(width), tok(width), tok(width), tok(LANES),
                   trn(width), trn(vrows), trn(width), trn(width), trn(IDX_HEADS)),
        compiler_params=_cparams(("parallel", "arbitrary")),
        name="in_proj",
    )(x, mod.reshape(bsz, 1, 3 * d), norm_g.reshape(1, d), w_nat, w_t, w_w, mblk, kg, qg)


def _s5_kernel(u_ref, zs_ref, bblk_ref, cre_ref, cim_ref, ep_re, ep_im, em_re, em_im,
               laml_re, laml_im, tri_ref, dsk_ref, wglu_ref, bglu_ref,
               xres_ref, ya_ref, w1_ref, w2_ref, mod_ref,
               o_ref, car_re, car_im, w_ref, x_ref, *, n_blocks, d_model):
    tm = u_ref.shape[1]
    nch = tm // SSM_CHUNK
    nst = cre_ref.shape[1]

    @pl.when(pl.program_id(1) == 0)
    def _():
        car_re[...] = jnp.zeros_like(car_re)
        car_im[...] = jnp.zeros_like(car_im)

    tri = tri_ref[...]
    for j in range(n_blocks):
        st = slice(j * nst, (j + 1) * nst)
        ch = slice(j * SSM_BLOCK_CH, (j + 1) * SSM_BLOCK_CH)
        p = jnp.dot(u_ref[0, :, ch], bblk_ref[j], preferred_element_type=F32)
        p_re = p[:, :nst].astype(BF16).reshape(nch, SSM_CHUNK, nst)
        p_im = p[:, nst:].astype(BF16).reshape(nch, SSM_CHUNK, nst)
        emr = em_re[:, st][None]
        emi = em_im[:, st][None]
        w_ref[j, :, 0:nst] = (p_re * emr - p_im * emi).reshape(tm, nst)
        w_ref[j, :, nst:2 * nst] = (p_re * emi + p_im * emr).reshape(tm, nst)
    for j in range(n_blocks):
        st = slice(j * nst, (j + 1) * nst)
        cum = jnp.dot(tri, w_ref[j], preferred_element_type=F32)
        cum_re = cum[:, :nst].reshape(nch, SSM_CHUNK, nst)
        cum_im = cum[:, nst:].reshape(nch, SSM_CHUNK, nst)
        lr = laml_re[:, st]
        li = laml_im[:, st]
        cr = car_re[:, st]
        ci = car_im[:, st]
        crs, cis = [], []
        for c in range(nch):
            crs.append(cr)
            cis.append(ci)
            tr_ = cum_re[c, SSM_CHUNK - 1:SSM_CHUNK, :] + cr
            ti_ = cum_im[c, SSM_CHUNK - 1:SSM_CHUNK, :] + ci
            cr = lr * tr_ - li * ti_
            ci = lr * ti_ + li * tr_
        car_re[:, st] = cr
        car_im[:, st] = ci
        a_re = (cum_re + jnp.stack(crs, axis=0)).astype(BF16)
        a_im = (cum_im + jnp.stack(cis, axis=0)).astype(BF16)
        epr = ep_re[:, st][None]
        epi = ep_im[:, st][None]
        x_ref[j, :, 0:nst] = (a_re * epr - a_im * epi).reshape(tm, nst)
        x_ref[j, :, nst:2 * nst] = (a_re * epi + a_im * epr).reshape(tm, nst)
    ys = []
    for j in range(n_blocks):
        ys.append(jnp.dot(x_ref[j, :, 0:nst], cre_ref[j], preferred_element_type=F32)
                  - jnp.dot(x_ref[j, :, nst:2 * nst], cim_ref[j], preferred_element_type=F32))
    y = jnp.concatenate(ys, axis=1) + dsk_ref[...] * u_ref[0].astype(F32)
    z = 0.5 * y * (1.0 + jnp.tanh(math.sqrt(2.0 / math.pi) * (y + 0.044715 * (y * y * y))))
    gl = jnp.dot(z.astype(BF16), wglu_ref[...], preferred_element_type=F32) + bglu_ref[...]
    zz = z * _sigmoid(gl)
    y_ssm = (zz * _silu(zs_ref[0].astype(F32))).astype(BF16)
    gate = mod_ref[0, :, 2 * d_model:3 * d_model]
    yo = (jnp.dot(y_ssm, w1_ref[...], preferred_element_type=F32)
          + jnp.dot(ya_ref[0], w2_ref[...], preferred_element_type=F32))
    o_ref[0] = xres_ref[0] + gate * yo


def _s5_mixer_out(u, zs, tabs, c_re, c_im, d_skip, w_glu, b_glu, x, y_attn, w_out, mod):
    bsz, seq, width = u.shape
    d = x.shape[-1]
    ep_re, ep_im, em_re, em_im, laml_re, laml_im, bb_re, bb_im = tabs
    tm = min(SSM_TILE, seq)
    gpb = SSM_BLOCK_CH // SSM_GROUP
    n_blocks = width // SSM_BLOCK_CH
    nst = gpb * SSM_STATE
    eye = jnp.eye(gpb, dtype=F32)

    def b_block(t):
        t4 = t.reshape(n_blocks, gpb, SSM_STATE, SSM_GROUP)
        return jnp.einsum('jgpc,gh->jgchp', t4, eye).reshape(n_blocks, SSM_BLOCK_CH, nst)

    def c_block(t):
        t4 = t.reshape(n_blocks, gpb, SSM_GROUP, SSM_STATE)
        return jnp.einsum('jgcp,gh->jhpgc', t4, eye).reshape(n_blocks, nst, SSM_BLOCK_CH)

    bblk = jnp.concatenate([b_block(bb_re), b_block(bb_im)], axis=2).astype(BF16)
    cre = c_block(c_re).astype(BF16)
    cim = c_block(c_im).astype(BF16)
    t_idx = np.arange(tm)
    tri = jnp.asarray(((t_idx[:, None] >= t_idx[None, :])
                       & (t_idx[:, None] // SSM_CHUNK == t_idx[None, :] // SSM_CHUNK)
                       ).astype(np.float32), BF16)
    n = ep_re.shape[1]
    const = lambda shape: pl.BlockSpec(shape, lambda b, s: (0,) * len(shape))
    tok = pl.BlockSpec((1, tm, width), lambda b, s: (b, s, 0))
    tokd = pl.BlockSpec((1, tm, d), lambda b, s: (b, s, 0))
    kern = functools.partial(_s5_kernel, n_blocks=n_blocks, d_model=d)
    return pl.pallas_call(
        kern,
        out_shape=jax.ShapeDtypeStruct((bsz, seq, d), x.dtype),
        grid=(bsz, seq // tm),
        in_specs=[tok, tok,
                  const((n_blocks, SSM_BLOCK_CH, 2 * nst)),
                  const((n_blocks, nst, SSM_BLOCK_CH)), const((n_blocks, nst, SSM_BLOCK_CH)),
                  const((SSM_CHUNK, n)), const((SSM_CHUNK, n)),
                  const((SSM_CHUNK, n)), const((SSM_CHUNK, n)),
                  const((1, n)), const((1, n)), const((tm, tm)),
                  const((1, width)), const((width, width)), const((1, width)),
                  tokd, tok, const((width, d)), const((width, d)),
                  pl.BlockSpec((1, 1, 3 * d), lambda b, s: (b, 0, 0))],
        out_specs=tokd,
        scratch_shapes=[pltpu.VMEM((1, n), F32), pltpu.VMEM((1, n), F32),
                        pltpu.VMEM((n_blocks, tm, 2 * nst), BF16),
                        pltpu.VMEM((n_blocks, tm, 2 * nst), BF16)],
        compiler_params=_cparams(("parallel", "arbitrary")),
        name="s5_mixer_out",
    )(u, zs, bblk, cre, cim, ep_re.astype(BF16), ep_im.astype(BF16),
      em_re.astype(BF16), em_im.astype(BF16), laml_re, laml_im, tri,
      d_skip.reshape(1, width), w_glu.astype(BF16), b_glu.reshape(1, width),
      x, y_attn, w_out[:width].astype(BF16), w_out[width:].astype(BF16),
      mod.reshape(bsz, 1, 3 * d))


def _dsa_kernel(k_ref, vt_ref, kidx_ref, qt_ref, qit_ref, azt_ref, wt_ref, tz_ref,
                o_ref, sc_ref, qip_ref, qp_ref, m_ref, acc_ref, mask_ref, s_ref, s2_ref,
                mt_ref, mt2_ref,
                p_ref, p2_ref, c_ref, c2_ref, sel_ref,
                *, topk, n_heads):
    t = ATT_TILE
    tq = qt_ref.shape[2]
    kpq = tq // t
    qi = pl.program_id(1)
    n_far = qi * kpq
    nk = n_far + kpq
    seq = k_ref.shape[1]
    row = lax.broadcasted_iota(jnp.int32, (t, tq), 0)
    colq = lax.broadcasted_iota(jnp.int32, (t, tq), 1)
    lane = lax.broadcasted_iota(jnp.int32, (1, tq), 1)
    pos = qi * tq + lane
    kq = jnp.minimum(pos + 1, topk).astype(F32)

    def tile_rows(kt):
        return pl.ds(pl.multiple_of(kt * t, t), t)

    zpad = jnp.zeros((HEAD_DIM, tq), BF16)
    for h in range(IDX_HEADS):
        qip_ref[h] = jnp.concatenate([qit_ref[0, h * IDX_DIM:(h + 1) * IDX_DIM, :], zpad], axis=0)
    for h in range(n_heads):
        qh = qt_ref[0, h * HEAD_DIM:(h + 1) * HEAD_DIM, :]
        qp_ref[h] = jnp.concatenate([qh, zpad] if h % 2 == 0 else [zpad, qh], axis=0)

    def score_tile(kt):
        kx = kidx_ref[0, tile_rows(kt), :]
        s = jnp.zeros((t, tq), F32)
        for h in range(IDX_HEADS):
            rel = jnp.dot(kx, qip_ref[h], preferred_element_type=F32)
            s = s + wt_ref[0, h:h + 1, :] * jnp.maximum(rel, 0.0)
        return s

    def score_far(kt, carry):
        smin, smax = carry
        s = score_tile(kt)
        sc_ref[tile_rows(kt), :] = s
        return (jnp.minimum(smin, jnp.min(s, axis=0, keepdims=True)),
                jnp.maximum(smax, jnp.max(s, axis=0, keepdims=True)))

    smin, smax = lax.fori_loop(
        0, n_far, score_far,
        (jnp.full((1, tq), jnp.inf, F32), jnp.full((1, tq), -jnp.inf, F32)))
    for d in range(kpq):
        s = score_tile(n_far + d)
        causal = (row + d * t) <= colq
        sc_ref[tile_rows(n_far + d), :] = jnp.where(causal, s, -jnp.inf)
        smin = jnp.minimum(smin, jnp.min(jnp.where(causal, s, jnp.inf), axis=0, keepdims=True))
        smax = jnp.maximum(smax, jnp.max(jnp.where(causal, s, -jnp.inf), axis=0, keepdims=True))

    def count_where(pred_fn):
        def body(kt, acc):
            m = pred_fn(sc_ref[tile_rows(kt), :])
            return acc + jnp.sum(jnp.where(m, 1.0, 0.0).reshape(4, t // 32, 8, tq), axis=1)
        acc = lax.fori_loop(0, nk, body, jnp.zeros((4, 8, tq), F32))
        return jnp.sum(acc.reshape(32, tq), axis=0, keepdims=True)

    def count_ge(thr):
        return count_where(lambda s_: s_ >= thr)

    def count_gt(thr):
        return count_where(lambda s_: s_ > thr)

    def min_where(pred_fn):
        def body(kt, acc):
            s = sc_ref[tile_rows(kt), :]
            return jnp.minimum(acc, jnp.min(jnp.where(pred_fn(s), s, jnp.inf), axis=0, keepdims=True))
        return lax.fori_loop(0, nk, body, jnp.full((1, tq), jnp.inf, F32))

    def bisect(st):
        lo, hi, c_lo = st
        mid = lo + 0.5 * (hi - lo)
        cnt = count_ge(mid)
        up = cnt >= kq
        return jnp.where(up, mid, lo), jnp.where(up, hi, mid), jnp.where(up, cnt, c_lo)

    c_ge0 = count_ge(0.0)
    c_gt0 = count_gt(0.0)
    above = c_gt0 >= kq
    below = c_ge0 < kq
    at_zero = jnp.logical_not(above | below)
    zero_tie = at_zero & (c_ge0 > kq)
    lo_neg = above & (smin <= 0.0)
    lo0 = jnp.where(at_zero | lo_neg, 0.0, smin)
    c_lo0 = jnp.where(at_zero | lo_neg, c_ge0, (pos + 1).astype(F32))
    hi0 = jnp.where(at_zero, 0.0, jnp.where(below, jnp.minimum(smax, 0.0), smax))
    open_f = jnp.where(zero_tie, 0.0, 1.0)

    st = lax.fori_loop(0, BISECT_MIN_ITERS, lambda _, s_: bisect(s_), (lo0, hi0, c_lo0))

    def more_cond(c):
        return (c[3] > 0.0) & (c[4] < BISECT_MAX_ITERS)

    def more_body(c):
        st_ = (c[0], c[1], c[2])
        flag = jnp.max((c[2] - kq) * open_f)
        return bisect(st_) + (flag, c[4] + 1)

    lo, _, c_lo, _, _ = lax.while_loop(
        more_cond, more_body, st + (jnp.float32(1.0), jnp.int32(BISECT_MIN_ITERS)))

    sel_ref[0:1, :] = lo
    sel_ref[1:2, :] = jnp.zeros((1, tq), F32)
    sel_ref[2:3, :] = kq - c_gt0
    sel_ref[3:4, :] = jnp.where(zero_tie, 1.0, 0.0)

    @pl.when(jnp.max((c_lo - kq) * open_f) > 0.0)
    def _():
        def peel_cond(c):
            return c[6] > 0.0

        def peel(c):
            lo_, c_lo_, taup_, ngt_, gtie_ = c[0], c[1], c[2], c[3], c[4]
            taup_n = min_where(lambda s_: s_ >= lo_)
            ngt_n = count_gt(taup_n)
            active = (c_lo_ > kq) & (open_f > 0.0) & (gtie_ == 0.0)
            is_tie = active & (ngt_n < kq)
            need_peel = active & (ngt_n >= kq)
            nxt = min_where(lambda s_: s_ > taup_n)
            return (jnp.where(need_peel, nxt, lo_), jnp.where(need_peel, ngt_n, c_lo_),
                    jnp.where(is_tie, taup_n, taup_), jnp.where(is_tie, ngt_n, ngt_),
                    jnp.where(is_tie, 1.0, gtie_), c[5],
                    jnp.max(jnp.where(need_peel, 1.0, 0.0)))

        zero = jnp.zeros((1, tq), F32)
        lo2, _, taup2, ngt2, gtie2, _, _ = lax.while_loop(
            peel_cond, peel, (lo, c_lo, zero, zero, zero, zero, jnp.float32(1.0)))
        gt = gtie2 > 0.0
        sel_ref[0:1, :] = lo2
        sel_ref[1:2, :] = jnp.where(gt, taup2, sel_ref[1:2, :])
        sel_ref[2:3, :] = jnp.where(gt, kq - ngt2, sel_ref[2:3, :])
        sel_ref[3:4, :] = jnp.where(gt, 1.0, sel_ref[3:4, :])

    lo_f = sel_ref[0:1, :]
    tie_f = sel_ref[3:4, :]
    any_tie = jnp.max(tie_f)

    @pl.when(any_tie <= 0.0)
    def _():
        def body(kt, c):
            s_ = sc_ref[tile_rows(kt), :]
            mask_ref[tile_rows(kt), :] = jnp.where(s_ >= lo_f, 0.0, NEG).astype(BF16)
            return c
        lax.fori_loop(0, nk, body, 0)

    @pl.when(any_tie > 0.0)
    def _():
        taup = sel_ref[1:2, :]
        need = jnp.where(tie_f > 0.0, sel_ref[2:3, :], float(2 * seq))
        tr_ = lax.broadcasted_iota(jnp.int32, (t, t), 0)
        tc_ = lax.broadcasted_iota(jnp.int32, (t, t), 1)
        tri = jnp.where(tc_ <= tr_, 1.0, 0.0).astype(BF16)

        def body(kt, seen):
            s_ = sc_ref[tile_rows(kt), :]
            z = s_ == taup
            pref = jnp.dot(tri, jnp.where(z, 1.0, 0.0).astype(BF16), preferred_element_type=F32)
            keep = (s_ >= lo_f) & jnp.logical_not(z & (seen + pref > need))
            mask_ref[tile_rows(kt), :] = jnp.where(keep, 0.0, NEG).astype(BF16)
            return seen + pref[t - 1:t, :]
        lax.fori_loop(0, nk, body, jnp.zeros((1, tq), F32))

    va = HEAD_DIM + VAUG

    def logits_head(h, kt, s_buf, mt_buf):
        rows = tile_rows(kt)
        pair = slice((h // 2) * 2 * HEAD_DIM, (h // 2 + 1) * 2 * HEAD_DIM)
        s = jnp.dot(k_ref[0, rows, pair], qp_ref[h], preferred_element_type=F32)
        s = s.astype(BF16) + mask_ref[rows, :]
        parts = []
        for g in range(tq // t):
            near = jnp.clip(n_far + g - kt, 0, 2)
            parts.append(s[:, g * t:(g + 1) * t] + tz_ref[near, h])
        s = jnp.concatenate(parts, axis=1) if len(parts) > 1 else parts[0]
        s_buf[h] = s
        mt_buf[h:h + 1, :] = jnp.max(s, axis=0, keepdims=True).astype(F32)

    def softmax_head(h, s_buf, mt_buf, p_buf, corr_buf):
        m_old = m_ref[h:h + 1, :]
        m_new = jnp.maximum(m_old, mt_buf[h:h + 1, :])
        corr_buf[h:h + 1, :] = jnp.exp2(m_old - m_new)
        p_buf[h] = jnp.exp2(s_buf[h] - m_new.astype(BF16))
        m_ref[h:h + 1, :] = m_new

    def pv_head(h, kt, p_buf, corr_buf):
        rows = tile_rows(kt)
        hs = slice(h * va, (h + 1) * va)
        acc_ref[hs, :] = acc_ref[hs, :] * corr_buf[h:h + 1, :] + jnp.dot(
            vt_ref[0, hs, rows], p_buf[h], preferred_element_type=F32)

    def logits_stage(kt, s_buf, mt_buf):
        for h in range(n_heads):
            logits_head(h, kt, s_buf, mt_buf)

    def pv_stage(kt, p_buf, corr_buf):
        for h in range(n_heads):
            pv_head(h, kt, p_buf, corr_buf)

    def trip(kt_pv, p_old, c_old, kt_next, s_next, mt_next, s_cur, mt_cur, p_cur, c_cur):
        for h in range(n_heads):
            pv_head(h, kt_pv, p_old, c_old)
            logits_head(h, kt_next, s_next, mt_next)
            softmax_head(h, s_cur, mt_cur, p_cur, c_cur)

    m_ref[...] = jnp.full_like(m_ref, NEG)
    acc_ref[...] = jnp.zeros_like(acc_ref)
    p2_ref[...] = jnp.zeros_like(p2_ref)
    c2_ref[...] = jnp.ones_like(c2_ref)
    logits_stage(0, s_ref, mt_ref)
    last = nk - 1

    def kv_body(j, c):
        i = 2 * j
        trip(jnp.maximum(i - 1, 0), p2_ref, c2_ref,
             jnp.minimum(i + 1, last), s2_ref, mt2_ref,
             s_ref, mt_ref, p_ref, c_ref)

        @pl.when(i + 1 < nk)
        def _():
            trip(i, p_ref, c_ref,
                 jnp.minimum(i + 2, last), s_ref, mt_ref,
                 s2_ref, mt2_ref, p2_ref, c2_ref)
        return c

    lax.fori_loop(0, (nk + 1) // 2, kv_body, 0)

    if kpq % 2 == 0:
        pv_stage(last, p2_ref, c2_ref)
    else:
        @pl.when(jnp.bitwise_and(nk, 1) == 1)
        def _():
            pv_stage(last, p_ref, c_ref)

        @pl.when(jnp.bitwise_and(nk, 1) == 0)
        def _():
            pv_stage(last, p2_ref, c2_ref)

    acc = acc_ref[...].reshape(n_heads, va, tq)
    out = acc[:, :HEAD_DIM, :] * (1.0 / acc[:, HEAD_DIM:HEAD_DIM + 1, :])
    out = out.reshape(n_heads * HEAD_DIM, tq) * _silu(azt_ref[0].astype(F32))
    o_ref[0] = out.T.astype(BF16)


def _dsa_mixer(k, vt, kidx, qt, qit, azt, wt, tz, topk):
    bsz, seq, width = k.shape
    n_heads = width // HEAD_DIM
    t = ATT_TILE
    tq = min(ATT_QUERIES, seq)
    per_b = lambda shape: pl.BlockSpec((1,) + shape, lambda b, q: (b, 0, 0))
    qtile = lambda r: pl.BlockSpec((1, r, tq), lambda b, q: (b, 0, q))
    kern = functools.partial(_dsa_kernel, topk=topk, n_heads=n_heads)
    vrows = vt.shape[1]
    return pl.pallas_call(
        kern,
        out_shape=jax.ShapeDtypeStruct((bsz, seq, width), BF16),
        grid=(bsz, seq // tq),
        in_specs=[per_b((seq, width)), per_b((vrows, seq)), per_b((seq, LANES)),
                  qtile(width), qtile(width), qtile(width), qtile(IDX_HEADS),
                  pl.BlockSpec((3, n_heads, t, t), lambda b, q: (0, 0, 0, 0))],
        out_specs=pl.BlockSpec((1, tq, width), lambda b, q: (b, q, 0)),
        scratch_shapes=[pltpu.VMEM((seq, tq), F32),
                        pltpu.VMEM((IDX_HEADS, 2 * IDX_DIM, tq), BF16),
                        pltpu.VMEM((n_heads, 2 * HEAD_DIM, tq), BF16),
                        pltpu.VMEM((n_heads, tq), F32),
                        pltpu.VMEM((vrows, tq), F32),
                        pltpu.VMEM((seq, tq), BF16),
                        pltpu.VMEM((n_heads, t, tq), BF16), pltpu.VMEM((n_heads, t, tq), BF16),
                        pltpu.VMEM((n_heads, tq), F32), pltpu.VMEM((n_heads, tq), F32),
                        pltpu.VMEM((n_heads, t, tq), BF16), pltpu.VMEM((n_heads, t, tq), BF16),
                        pltpu.VMEM((n_heads, tq), F32), pltpu.VMEM((n_heads, tq), F32),
                        pltpu.VMEM((8, tq), F32)],
        compiler_params=_cparams(("parallel", "arbitrary")),
        name="dsa_mixer",
    )(k, vt, kidx, qt, qit, azt, wt, tz)


def kernel(x, c, rel_bias, norm_g, w_ada, b_ada, w_in, q_gain, k_gain, a_re, a_im, log_dt,
           b_re, b_im, c_re, c_im, d_skip, w_glu, b_glu, w_out):
    bsz, seq, d = x.shape
    depth = w_in.shape[0]
    width = d // 2
    n_heads = width // HEAD_DIM
    topk = min(TOPK_MAX, seq // 4)
    assert seq % min(ATT_QUERIES, seq) == 0 and ATT_QUERIES % ATT_TILE == 0
    assert seq % ATT_TILE == 0 and seq % SSM_CHUNK == 0
    assert ATT_TILE + 1 >= MAX_DISTANCE
    tz = _bias_prep(rel_bias, n_heads)
    for l in range(depth):
        mod = _adaln_mod(c, w_ada[l], b_ada[l])
        u, zs, k, kidx, qt, vt, qit, azt, wt = _in_proj(x, mod, norm_g[l], w_in[l],
                                                        q_gain[l], k_gain[l])
        tabs = _ssm_prep(a_re[l], a_im[l], log_dt[l], b_re[l], b_im[l])
        y_attn = _dsa_mixer(k, vt, kidx, qt, qit, azt, wt, tz, topk)
        x = _s5_mixer_out(u, zs, tabs, c_re[l], c_im[l], d_skip[l], w_glu[l], b_glu[l],
                          x, y_attn, w_out[l], mod)
    return x
```

```python
import functools
import math

import jax
import jax.numpy as jnp
import numpy as np
from jax import lax
from jax.experimental import pallas as pl
from jax.experimental.pallas import tpu as pltpu

F32 = jnp.float32
BF16 = jnp.bfloat16

SSM_GROUP = 16
SSM_STATE = 64
HEAD_DIM = 64
VAUG = 16
IDX_HEADS = 8
IDX_DIM = 64
TOPK_MAX = 256
NUM_BUCKETS = 32
MAX_DISTANCE = 128
EPS = 1e-6

LANES = 128
V7X_VMEM_LIMIT_BYTES = 56 * 1024 * 1024

SSM_CHUNK = 128
SSM_TILE = 512
SSM_SUB = 256
SSM_BLOCK_CH = 128
PROJ_TILE = 512
ATT_TILE = 256
ATT_QUERIES = 512
BISECT_MIN_ITERS = 14
BISECT_MAX_ITERS = 26
NEG = -(2.0 ** 100)
LOG2E = math.log2(math.e)

NT_DIMS = (((1,), (1,)), ((), ()))


def _cparams(sem):
    return pltpu.CompilerParams(dimension_semantics=sem,
                                vmem_limit_bytes=V7X_VMEM_LIMIT_BYTES)


def _sigmoid(x):
    return 1.0 / (1.0 + jnp.exp(-x))


def _silu(x):
    return x * _sigmoid(x)


def _ssm_prep_kernel(are_r, aim_r, ldt_r, are_c, aim_c, ldt_c, bre, bim,
                     ep_re, ep_im, em_re, em_im, laml_re, laml_im, bb_re, bb_im):
    dt = jnp.exp(ldt_r[...])
    ar = are_r[...] * dt
    ai = aim_r[...] * dt
    n = ar.shape[-1]
    tau = lax.broadcasted_iota(jnp.int32, (SSM_CHUNK, n), 0).astype(F32)
    mag_p = jnp.exp(tau * ar)
    mag_m = jnp.exp(-tau * ar)
    cs = jnp.cos(tau * ai)
    sn = jnp.sin(tau * ai)
    ep_re[...] = mag_p * cs
    ep_im[...] = mag_p * sn
    em_re[...] = mag_m * cs
    em_im[...] = -(mag_m * sn)
    mag_l = jnp.exp(float(SSM_CHUNK) * ar)
    laml_re[...] = mag_l * jnp.cos(float(SSM_CHUNK) * ai)
    laml_im[...] = mag_l * jnp.sin(float(SSM_CHUNK) * ai)
    a = are_c[...]
    b = aim_c[...]
    dtc = jnp.exp(ldt_c[...])
    mag = jnp.exp(a * dtc)
    x = mag * jnp.cos(b * dtc) - 1.0
    y = mag * jnp.sin(b * dtc)
    den = a * a + b * b
    cre = (x * a + y * b) / den
    cim = (y * a - x * b) / den
    bb_re[...] = cre * bre[...] - cim * bim[...]
    bb_im[...] = cre * bim[...] + cim * bre[...]


def _ssm_prep(a_re, a_im, log_dt, b_re, b_im):
    g, p = a_re.shape
    n = g * p
    ldt = jnp.broadcast_to(log_dt[:, None], (g, p))
    row = lambda t: t.reshape(1, n)
    col = lambda t: t.reshape(n, 1)
    tab = jax.ShapeDtypeStruct((SSM_CHUNK, n), F32)
    vec = jax.ShapeDtypeStruct((1, n), F32)
    bsh = jax.ShapeDtypeStruct((n, SSM_GROUP), F32)
    return pl.pallas_call(
        _ssm_prep_kernel,
        out_shape=(tab, tab, tab, tab, vec, vec, bsh, bsh),
        name="ssm_prep",
    )(row(a_re), row(a_im), row(ldt), col(a_re), col(a_im), col(ldt),
      b_re.reshape(n, SSM_GROUP), b_im.reshape(n, SSM_GROUP))


def _bias_prep_kernel(rb_ref, o_ref):
    t = o_ref.shape[-1]
    n_heads = o_ref.shape[1]
    i = lax.broadcasted_iota(jnp.int32, (t, t), 0)
    j = lax.broadcasted_iota(jnp.int32, (t, t), 1)
    max_exact = NUM_BUCKETS // 2

    def bias(idx, h):
        dist = jnp.maximum(idx * t + j - i, 0)
        d = jnp.maximum(dist, 1).astype(F32)
        large = max_exact + (jnp.log(d / max_exact) / math.log(MAX_DISTANCE / max_exact)
                             * (NUM_BUCKETS - max_exact)).astype(jnp.int32)
        large = jnp.minimum(large, NUM_BUCKETS - 1)
        bucket = jnp.where(dist < max_exact, dist, large)
        acc = jnp.zeros((t, t), F32)
        for b in range(NUM_BUCKETS):
            acc = jnp.where(bucket == b, rb_ref[b, h], acc)
        return acc

    for h in range(n_heads):
        far = bias(2, h)
        for idx in range(3):
            o_ref[idx, h] = ((bias(idx, h) - far) * LOG2E).astype(o_ref.dtype)


def _bias_prep(rel_bias, n_heads):
    t = ATT_TILE
    return pl.pallas_call(
        _bias_prep_kernel,
        out_shape=jax.ShapeDtypeStruct((3, n_heads, t, t), BF16),
        in_specs=[pl.BlockSpec(memory_space=pltpu.SMEM)],
        name="bias_prep",
    )(rel_bias)


def _mod_kernel(c_ref, w_ref, b_ref, o_ref):
    c = c_ref[...]
    cond = _silu(c)
    o_ref[...] = jnp.dot(cond.astype(BF16), w_ref[...].astype(BF16),
                         preferred_element_type=F32) + b_ref[...]


def _adaln_mod(c, w_ada, b_ada):
    bsz, d = c.shape
    n = w_ada.shape[1]
    tn = 512
    return pl.pallas_call(
        _mod_kernel,
        out_shape=jax.ShapeDtypeStruct((bsz, n), F32),
        grid=(n // tn,),
        in_specs=[pl.BlockSpec((bsz, d), lambda i: (0, 0)),
                  pl.BlockSpec((d, tn), lambda i: (0, i)),
                  pl.BlockSpec((1, tn), lambda i: (0, i))],
        out_specs=pl.BlockSpec((bsz, tn), lambda i: (0, i)),
        compiler_params=_cparams(("arbitrary",)),
        name="adaln_mod",
    )(c, w_ada, b_ada.reshape(1, n))


def _in_proj_kernel(x_ref, mod_ref, g_ref, wnat_ref, wt_ref, ww_ref, mblk_ref,
                    kg_ref, qg_ref,
                    u_ref, zs_ref, k_ref, kidx_ref, qt_ref, vt_ref, qit_ref, azt_ref, wt_out_ref,
                    *, d_model, width, n_heads):
    x = x_ref[0]
    shift = mod_ref[0, :, 0:d_model]
    scale = mod_ref[0, :, d_model:2 * d_model]
    ms = jnp.mean(x * x, axis=-1, keepdims=True)
    xn = x * lax.rsqrt(ms + EPS) * g_ref[...]
    h = (xn * (1.0 + scale) + shift).astype(BF16)
    tm = h.shape[0]

    nat = jnp.dot(h, wnat_ref[...], preferred_element_type=F32)
    u_ref[0] = nat[:, 0:width].astype(BF16)
    zs_ref[0] = nat[:, width:2 * width].astype(BF16)
    k = nat[:, 2 * width:3 * width]
    kms = jnp.dot((k * k).astype(BF16), mblk_ref[...], preferred_element_type=F32)
    k_ref[0] = (k * lax.rsqrt(kms + EPS) * kg_ref[...]).astype(BF16)
    kidx_ref[0] = nat[:, 3 * width:3 * width + LANES].astype(BF16)

    tr = lax.dot_general(wt_ref[...], h, NT_DIMS, preferred_element_type=F32)
    q3 = tr[0:width].reshape(n_heads, HEAD_DIM, tm)
    qms = jnp.mean(q3 * q3, axis=1, keepdims=True)
    qn = q3 * lax.rsqrt(qms + EPS) * qg_ref[...][None] * (HEAD_DIM ** -0.5 * LOG2E)
    qt_ref[0] = qn.reshape(width, tm).astype(BF16)
    v3 = tr[width:2 * width].astype(BF16).reshape(n_heads, HEAD_DIM, tm)
    aug = lax.broadcasted_iota(jnp.int32, (n_heads, VAUG, tm), 1)
    aug = jnp.where(aug == 0, 1.0, 0.0).astype(BF16)
    vt_ref[0] = jnp.concatenate([v3, aug], axis=1).reshape(n_heads * (HEAD_DIM + VAUG), tm)
    qit_ref[0] = tr[2 * width:3 * width].astype(BF16)
    azt_ref[0] = tr[3 * width:4 * width].astype(BF16)
    wt_out_ref[0] = lax.dot_general(ww_ref[...], h, NT_DIMS,
                                    preferred_element_type=F32) * (IDX_HEADS ** -0.5)


def _in_proj(x, mod, norm_g, w_in, q_gain, k_gain):
    bsz, seq, d = x.shape
    width = d // 2
    n_heads = width // HEAD_DIM
    tm = min(PROJ_TILE, seq)
    o = np.cumsum([0, width, width, width, width, width, width,
                   IDX_HEADS * IDX_DIM, IDX_DIM, IDX_HEADS])
    col = lambda i: w_in[:, int(o[i]):int(o[i + 1])]
    w_kidx = jnp.pad(col(7), ((0, 0), (0, LANES - IDX_DIM)))
    w_nat = jnp.concatenate([col(0), col(1), col(3), w_kidx], axis=1).astype(BF16)
    w_t = jnp.concatenate([col(2), col(4), col(6), col(5)], axis=1).T.astype(BF16)
    w_w = col(8).T.astype(BF16)
    hid = np.arange(width) // HEAD_DIM
    mblk = jnp.asarray((hid[:, None] == hid[None, :]).astype(np.float32) / HEAD_DIM, BF16)
    kg = jnp.tile(k_gain, n_heads).reshape(1, width)
    qg = q_gain.reshape(HEAD_DIM, 1)

    nat_w = w_nat.shape[1]
    bs = jax.ShapeDtypeStruct
    const = lambda shape: pl.BlockSpec(shape, lambda b, s: (0,) * len(shape))
    tok = lambda w: pl.BlockSpec((1, tm, w), lambda b, s: (b, s, 0))
    trn = lambda r: pl.BlockSpec((1, r, tm), lambda b, s: (b, 0, s))
    kern = functools.partial(_in_proj_kernel, d_model=d, width=width, n_heads=n_heads)
    vrows = n_heads * (HEAD_DIM + VAUG)
    return pl.pallas_call(
        kern,
        out_shape=(bs((bsz, seq, width), BF16), bs((bsz, seq, width), BF16),
                   bs((bsz, seq, width), BF16), bs((bsz, seq, LANES), BF16),
                   bs((bsz, width, seq), BF16), bs((bsz, vrows, seq), BF16),
                   bs((bsz, width, seq), BF16), bs((bsz, width, seq), BF16),
                   bs((bsz, IDX_HEADS, seq), F32)),
        grid=(bsz, seq // tm),
        in_specs=[tok(d),
                  pl.BlockSpec((1, 1, 3 * d), lambda b, s: (b, 0, 0)),
                  const((1, d)), const((d, nat_w)), const((4 * width, d)),
                  const((IDX_HEADS, d)), const((width, width)),
                  const((1, width)), const((HEAD_DIM, 1))],
        out_specs=(tok(width), tok(width), tok(width), tok(LANES),
                   trn(width), trn(vrows), trn(width), trn(width), trn(IDX_HEADS)),
        compiler_params=_cparams(("parallel", "arbitrary")),
        name="in_proj",
    )(x, mod.reshape(bsz, 1, 3 * d), norm_g.reshape(1, d), w_nat, w_t, w_w, mblk, kg, qg)


def _s5_kernel(u_ref, zs_ref, bblk_ref, cre_ref, cim_ref, ep_re, ep_im, em_re, em_im,
               laml_re, laml_im, tri_ref, dsk_ref, wglu_ref, bglu_ref,
               xres_ref, ya_ref, w1_ref, w2_ref, mod_ref,
               o_ref, car_re, car_im, w_ref, x_ref, *, n_blocks, d_model):
    tm = u_ref.shape[1]
    sub = tri_ref.shape[0]
    nch = sub // SSM_CHUNK
    nst = cre_ref.shape[1]

    @pl.when(pl.program_id(1) == 0)
    def _():
        car_re[...] = jnp.zeros_like(car_re)
        car_im[...] = jnp.zeros_like(car_im)

    tri = tri_ref[...]
    halves = [slice(r * sub, (r + 1) * sub) for r in range(tm // sub)]
    for rs in halves:
        for j in range(n_blocks):
            st = slice(j * nst, (j + 1) * nst)
            ch = slice(j * SSM_BLOCK_CH, (j + 1) * SSM_BLOCK_CH)
            p = jnp.dot(u_ref[0, rs, ch], bblk_ref[j], preferred_element_type=F32)
            p_re = p[:, :nst].astype(BF16).reshape(nch, SSM_CHUNK, nst)
            p_im = p[:, nst:].astype(BF16).reshape(nch, SSM_CHUNK, nst)
            emr = em_re[:, st][None]
            emi = em_im[:, st][None]
            w_ref[j, rs, 0:nst] = (p_re * emr - p_im * emi).reshape(sub, nst)
            w_ref[j, rs, nst:2 * nst] = (p_re * emi + p_im * emr).reshape(sub, nst)
    for rs in halves:
        for j in range(n_blocks):
            st = slice(j * nst, (j + 1) * nst)
            cum = jnp.dot(tri, w_ref[j, rs, :], preferred_element_type=F32)
            cum_re = cum[:, :nst].reshape(nch, SSM_CHUNK, nst)
            cum_im = cum[:, nst:].reshape(nch, SSM_CHUNK, nst)
            lr = laml_re[:, st]
            li = laml_im[:, st]
            cr = car_re[:, st]
            ci = car_im[:, st]
            crs, cis = [], []
            for c in range(nch):
                crs.append(cr)
                cis.append(ci)
                tr_ = cum_re[c, SSM_CHUNK - 1:SSM_CHUNK, :] + cr
                ti_ = cum_im[c, SSM_CHUNK - 1:SSM_CHUNK, :] + ci
                cr = lr * tr_ - li * ti_
                ci = lr * ti_ + li * tr_
            car_re[:, st] = cr
            car_im[:, st] = ci
            a_re = (cum_re + jnp.stack(crs, axis=0)).astype(BF16)
            a_im = (cum_im + jnp.stack(cis, axis=0)).astype(BF16)
            epr = ep_re[:, st][None]
            epi = ep_im[:, st][None]
            x_ref[j, rs, 0:nst] = (a_re * epr - a_im * epi).reshape(sub, nst)
            x_ref[j, rs, nst:2 * nst] = (a_re * epi + a_im * epr).reshape(sub, nst)
    ys = []
    for j in range(n_blocks):
        ys.append(jnp.dot(x_ref[j, :, 0:nst], cre_ref[j], preferred_element_type=F32)
                  - jnp.dot(x_ref[j, :, nst:2 * nst], cim_ref[j], preferred_element_type=F32))
    y = jnp.concatenate(ys, axis=1) + dsk_ref[...] * u_ref[0].astype(F32)
    z = 0.5 * y * (1.0 + jnp.tanh(math.sqrt(2.0 / math.pi) * (y + 0.044715 * (y * y * y))))
    gl = jnp.dot(z.astype(BF16), wglu_ref[...], preferred_element_type=F32) + bglu_ref[...]
    zz = z * _sigmoid(gl)
    y_ssm = (zz * _silu(zs_ref[0].astype(F32))).astype(BF16)
    gate = mod_ref[0, :, 2 * d_model:3 * d_model]
    yo = (jnp.dot(y_ssm, w1_ref[...], preferred_element_type=F32)
          + jnp.dot(ya_ref[0], w2_ref[...], preferred_element_type=F32))
    o_ref[0] = xres_ref[0] + gate * yo


def _s5_mixer_out(u, zs, tabs, c_re, c_im, d_skip, w_glu, b_glu, x, y_attn, w_out, mod):
    bsz, seq, width = u.shape
    d = x.shape[-1]
    ep_re, ep_im, em_re, em_im, laml_re, laml_im, bb_re, bb_im = tabs
    tm = min(SSM_TILE, seq)
    gpb = SSM_BLOCK_CH // SSM_GROUP
    n_blocks = width // SSM_BLOCK_CH
    nst = gpb * SSM_STATE
    eye = jnp.eye(gpb, dtype=F32)

    def b_block(t):
        t4 = t.reshape(n_blocks, gpb, SSM_STATE, SSM_GROUP)
        return jnp.einsum('jgpc,gh->jgchp', t4, eye).reshape(n_blocks, SSM_BLOCK_CH, nst)

    def c_block(t):
        t4 = t.reshape(n_blocks, gpb, SSM_GROUP, SSM_STATE)
        return jnp.einsum('jgcp,gh->jhpgc', t4, eye).reshape(n_blocks, nst, SSM_BLOCK_CH)

    bblk = jnp.concatenate([b_block(bb_re), b_block(bb_im)], axis=2).astype(BF16)
    cre = c_block(c_re).astype(BF16)
    cim = c_block(c_im).astype(BF16)
    sub = min(SSM_SUB, tm)
    t_idx = np.arange(sub)
    tri = jnp.asarray(((t_idx[:, None] >= t_idx[None, :])
                       & (t_idx[:, None] // SSM_CHUNK == t_idx[None, :] // SSM_CHUNK)
                       ).astype(np.float32), BF16)
    n = ep_re.shape[1]
    const = lambda shape: pl.BlockSpec(shape, lambda b, s: (0,) * len(shape))
    tok = pl.BlockSpec((1, tm, width), lambda b, s: (b, s, 0))
    tokd = pl.BlockSpec((1, tm, d), lambda b, s: (b, s, 0))
    kern = functools.partial(_s5_kernel, n_blocks=n_blocks, d_model=d)
    return pl.pallas_call(
        kern,
        out_shape=jax.ShapeDtypeStruct((bsz, seq, d), x.dtype),
        grid=(bsz, seq // tm),
        in_specs=[tok, tok,
                  const((n_blocks, SSM_BLOCK_CH, 2 * nst)),
                  const((n_blocks, nst, SSM_BLOCK_CH)), const((n_blocks, nst, SSM_BLOCK_CH)),
                  const((SSM_CHUNK, n)), const((SSM_CHUNK, n)),
                  const((SSM_CHUNK, n)), const((SSM_CHUNK, n)),
                  const((1, n)), const((1, n)), const((sub, sub)),
                  const((1, width)), const((width, width)), const((1, width)),
                  tokd, tok, const((width, d)), const((width, d)),
                  pl.BlockSpec((1, 1, 3 * d), lambda b, s: (b, 0, 0))],
        out_specs=tokd,
        scratch_shapes=[pltpu.VMEM((1, n), F32), pltpu.VMEM((1, n), F32),
                        pltpu.VMEM((n_blocks, tm, 2 * nst), BF16),
                        pltpu.VMEM((n_blocks, tm, 2 * nst), BF16)],
        compiler_params=_cparams(("parallel", "arbitrary")),
        name="s5_mixer_out",
    )(u, zs, bblk, cre, cim, ep_re.astype(BF16), ep_im.astype(BF16),
      em_re.astype(BF16), em_im.astype(BF16), laml_re, laml_im, tri,
      d_skip.reshape(1, width), w_glu.astype(BF16), b_glu.reshape(1, width),
      x, y_attn, w_out[:width].astype(BF16), w_out[width:].astype(BF16),
      mod.reshape(bsz, 1, 3 * d))


def _dsa_kernel(k_ref, vt_ref, kidx_ref, qt_ref, qit_ref, azt_ref, wt_ref, tz_ref,
                o_ref, sc_ref, qip_ref, qp_ref, m_ref, acc_ref, mask_ref, s_ref, s2_ref,
                mt_ref, mt2_ref,
                p_ref, p2_ref, c_ref, c2_ref, sel_ref,
                *, topk, n_heads):
    t = ATT_TILE
    tq = qt_ref.shape[2]
    kpq = tq // t
    qi = pl.program_id(1)
    n_far = qi * kpq
    nk = n_far + kpq
    seq = k_ref.shape[1]
    row = lax.broadcasted_iota(jnp.int32, (t, tq), 0)
    colq = lax.broadcasted_iota(jnp.int32, (t, tq), 1)
    lane = lax.broadcasted_iota(jnp.int32, (1, tq), 1)
    pos = qi * tq + lane
    kq = jnp.minimum(pos + 1, topk).astype(F32)

    def tile_rows(kt):
        return pl.ds(pl.multiple_of(kt * t, t), t)

    zpad = jnp.zeros((HEAD_DIM, tq), BF16)
    for h in range(IDX_HEADS):
        qip_ref[h] = jnp.concatenate([qit_ref[0, h * IDX_DIM:(h + 1) * IDX_DIM, :], zpad], axis=0)
    for h in range(n_heads):
        qh = qt_ref[0, h * HEAD_DIM:(h + 1) * HEAD_DIM, :]
        qp_ref[h] = jnp.concatenate([qh, zpad] if h % 2 == 0 else [zpad, qh], axis=0)

    def score_tile(kt):
        kx = kidx_ref[0, tile_rows(kt), :]
        s = jnp.zeros((t, tq), F32)
        for h in range(IDX_HEADS):
            rel = jnp.dot(kx, qip_ref[h], preferred_element_type=F32)
            s = s + wt_ref[0, h:h + 1, :] * jnp.maximum(rel, 0.0)
        return s

    def score_far(kt, carry):
        smin, smax = carry
        s = score_tile(kt)
        sc_ref[tile_rows(kt), :] = s
        return (jnp.minimum(smin, jnp.min(s, axis=0, keepdims=True)),
                jnp.maximum(smax, jnp.max(s, axis=0, keepdims=True)))

    smin, smax = lax.fori_loop(
        0, n_far, score_far,
        (jnp.full((1, tq), jnp.inf, F32), jnp.full((1, tq), -jnp.inf, F32)))
    for d in range(kpq):
        s = score_tile(n_far + d)
        causal = (row + d * t) <= colq
        sc_ref[tile_rows(n_far + d), :] = jnp.where(causal, s, -jnp.inf)
        smin = jnp.minimum(smin, jnp.min(jnp.where(causal, s, jnp.inf), axis=0, keepdims=True))
        smax = jnp.maximum(smax, jnp.max(jnp.where(causal, s, -jnp.inf), axis=0, keepdims=True))

    def count_where(pred_fn):
        def body(j, acc):
            for d in range(kpq):
                m = pred_fn(sc_ref[tile_rows(j * kpq + d), :])
                acc = acc + jnp.sum(jnp.where(m, 1.0, 0.0).reshape(4, t // 32, 8, tq), axis=1)
            return acc
        acc = lax.fori_loop(0, qi + 1, body, jnp.zeros((4, 8, tq), F32))
        return jnp.sum(acc.reshape(32, tq), axis=0, keepdims=True)

    def count_ge(thr):
        return count_where(lambda s_: s_ >= thr)

    def count_gt(thr):
        return count_where(lambda s_: s_ > thr)

    def min_where(pred_fn):
        def body(kt, acc):
            s = sc_ref[tile_rows(kt), :]
            return jnp.minimum(acc, jnp.min(jnp.where(pred_fn(s), s, jnp.inf), axis=0, keepdims=True))
        return lax.fori_loop(0, nk, body, jnp.full((1, tq), jnp.inf, F32))

    def bisect(st):
        lo, hi, c_lo = st
        mid = lo + 0.5 * (hi - lo)
        cnt = count_ge(mid)
        up = cnt >= kq
        return jnp.where(up, mid, lo), jnp.where(up, hi, mid), jnp.where(up, cnt, c_lo)

    c_ge0 = count_ge(0.0)
    c_gt0 = count_gt(0.0)
    above = c_gt0 >= kq
    below = c_ge0 < kq
    at_zero = jnp.logical_not(above | below)
    zero_tie = at_zero & (c_ge0 > kq)
    lo_neg = above & (smin <= 0.0)
    lo0 = jnp.where(at_zero | lo_neg, 0.0, smin)
    c_lo0 = jnp.where(at_zero | lo_neg, c_ge0, (pos + 1).astype(F32))
    hi0 = jnp.where(at_zero, 0.0, jnp.where(below, jnp.minimum(smax, 0.0), smax))
    open_f = jnp.where(zero_tie, 0.0, 1.0)

    st = lax.fori_loop(0, BISECT_MIN_ITERS, lambda _, s_: bisect(s_), (lo0, hi0, c_lo0))

    def more_cond(c):
        return (c[3] > 0.0) & (c[4] < BISECT_MAX_ITERS)

    def more_body(c):
        st_ = (c[0], c[1], c[2])
        flag = jnp.max((c[2] - kq) * open_f)
        return bisect(st_) + (flag, c[4] + 1)

    lo, _, c_lo, _, _ = lax.while_loop(
        more_cond, more_body, st + (jnp.float32(1.0), jnp.int32(BISECT_MIN_ITERS)))

    sel_ref[0:1, :] = lo
    sel_ref[1:2, :] = jnp.zeros((1, tq), F32)
    sel_ref[2:3, :] = kq - c_gt0
    sel_ref[3:4, :] = jnp.where(zero_tie, 1.0, 0.0)

    @pl.when(jnp.max((c_lo - kq) * open_f) > 0.0)
    def _():
        def peel_cond(c):
            return c[6] > 0.0

        def peel(c):
            lo_, c_lo_, taup_, ngt_, gtie_ = c[0], c[1], c[2], c[3], c[4]
            taup_n = min_where(lambda s_: s_ >= lo_)
            ngt_n = count_gt(taup_n)
            active = (c_lo_ > kq) & (open_f > 0.0) & (gtie_ == 0.0)
            is_tie = active & (ngt_n < kq)
            need_peel = active & (ngt_n >= kq)
            nxt = min_where(lambda s_: s_ > taup_n)
            return (jnp.where(need_peel, nxt, lo_), jnp.where(need_peel, ngt_n, c_lo_),
                    jnp.where(is_tie, taup_n, taup_), jnp.where(is_tie, ngt_n, ngt_),
                    jnp.where(is_tie, 1.0, gtie_), c[5],
                    jnp.max(jnp.where(need_peel, 1.0, 0.0)))

        zero = jnp.zeros((1, tq), F32)
        lo2, _, taup2, ngt2, gtie2, _, _ = lax.while_loop(
            peel_cond, peel, (lo, c_lo, zero, zero, zero, zero, jnp.float32(1.0)))
        gt = gtie2 > 0.0
        sel_ref[0:1, :] = lo2
        sel_ref[1:2, :] = jnp.where(gt, taup2, sel_ref[1:2, :])
        sel_ref[2:3, :] = jnp.where(gt, kq - ngt2, sel_ref[2:3, :])
        sel_ref[3:4, :] = jnp.where(gt, 1.0, sel_ref[3:4, :])

    lo_f = sel_ref[0:1, :]
    tie_f = sel_ref[3:4, :]
    any_tie = jnp.max(tie_f)

    @pl.when(any_tie <= 0.0)
    def _():
        def body(kt, c):
            s_ = sc_ref[tile_rows(kt), :]
            mask_ref[tile_rows(kt), :] = jnp.where(s_ >= lo_f, 0.0, NEG).astype(BF16)
            return c
        lax.fori_loop(0, nk, body, 0)

    @pl.when(any_tie > 0.0)
    def _():
        taup = sel_ref[1:2, :]
        need = jnp.where(tie_f > 0.0, sel_ref[2:3, :], float(2 * seq))
        tr_ = lax.broadcasted_iota(jnp.int32, (t, t), 0)
        tc_ = lax.broadcasted_iota(jnp.int32, (t, t), 1)
        tri = jnp.where(tc_ <= tr_, 1.0, 0.0).astype(BF16)

        def body(kt, seen):
            s_ = sc_ref[tile_rows(kt), :]
            z = s_ == taup
            pref = jnp.dot(tri, jnp.where(z, 1.0, 0.0).astype(BF16), preferred_element_type=F32)
            keep = (s_ >= lo_f) & jnp.logical_not(z & (seen + pref > need))
            mask_ref[tile_rows(kt), :] = jnp.where(keep, 0.0, NEG).astype(BF16)
            return seen + pref[t - 1:t, :]
        lax.fori_loop(0, nk, body, jnp.zeros((1, tq), F32))

    va = HEAD_DIM + VAUG

    def logits_head(h, kt, s_buf, mt_buf):
        rows = tile_rows(kt)
        pair = slice((h // 2) * 2 * HEAD_DIM, (h // 2 + 1) * 2 * HEAD_DIM)
        s = jnp.dot(k_ref[0, rows, pair], qp_ref[h], preferred_element_type=F32)
        s = s.astype(BF16) + mask_ref[rows, :]
        parts = []
        for g in range(tq // t):
            near = jnp.clip(n_far + g - kt, 0, 2)
            parts.append(s[:, g * t:(g + 1) * t] + tz_ref[near, h])
        s = jnp.concatenate(parts, axis=1) if len(parts) > 1 else parts[0]
        s_buf[h] = s
        mt_buf[h:h + 1, :] = jnp.max(s, axis=0, keepdims=True).astype(F32)

    def softmax_head(h, s_buf, mt_buf, p_buf, corr_buf):
        m_old = m_ref[h:h + 1, :]
        m_new = jnp.maximum(m_old, mt_buf[h:h + 1, :])
        corr_buf[h:h + 1, :] = jnp.exp2(m_old - m_new)
        p_buf[h] = jnp.exp2(s_buf[h] - m_new.astype(BF16))
        m_ref[h:h + 1, :] = m_new

    def pv_head(h, kt, p_buf, corr_buf):
        rows = tile_rows(kt)
        hs = slice(h * va, (h + 1) * va)
        acc_ref[hs, :] = acc_ref[hs, :] * corr_buf[h:h + 1, :] + jnp.dot(
            vt_ref[0, hs, rows], p_buf[h], preferred_element_type=F32)

    def logits_stage(kt, s_buf, mt_buf):
        for h in range(n_heads):
            logits_head(h, kt, s_buf, mt_buf)

    def pv_stage(kt, p_buf, corr_buf):
        for h in range(n_heads):
            pv_head(h, kt, p_buf, corr_buf)

    def trip(kt_pv, p_old, c_old, kt_next, s_next, mt_next, s_cur, mt_cur, p_cur, c_cur):
        for h in range(n_heads):
            pv_head(h, kt_pv, p_old, c_old)
            logits_head(h, kt_next, s_next, mt_next)
            softmax_head(h, s_cur, mt_cur, p_cur, c_cur)

    m_ref[...] = jnp.full_like(m_ref, NEG)
    acc_ref[...] = jnp.zeros_like(acc_ref)
    p2_ref[...] = jnp.zeros_like(p2_ref)
    c2_ref[...] = jnp.ones_like(c2_ref)
    logits_stage(0, s_ref, mt_ref)
    last = nk - 1

    def kv_body(j, c):
        i = 2 * j
        trip(jnp.maximum(i - 1, 0), p2_ref, c2_ref,
             jnp.minimum(i + 1, last), s2_ref, mt2_ref,
             s_ref, mt_ref, p_ref, c_ref)

        @pl.when(i + 1 < nk)
        def _():
            trip(i, p_ref, c_ref,
                 jnp.minimum(i + 2, last), s_ref, mt_ref,
                 s2_ref, mt2_ref, p2_ref, c2_ref)
        return c

    lax.fori_loop(0, (nk + 1) // 2, kv_body, 0)

    if kpq % 2 == 0:
        pv_stage(last, p2_ref, c2_ref)
    else:
        @pl.when(jnp.bitwise_and(nk, 1) == 1)
        def _():
            pv_stage(last, p_ref, c_ref)

        @pl.when(jnp.bitwise_and(nk, 1) == 0)
        def _():
            pv_stage(last, p2_ref, c2_ref)

    acc = acc_ref[...].reshape(n_heads, va, tq)
    out = acc[:, :HEAD_DIM, :] * (1.0 / acc[:, HEAD_DIM:HEAD_DIM + 1, :])
    out = out.reshape(n_heads * HEAD_DIM, tq) * _silu(azt_ref[0].astype(F32))
    o_ref[0] = out.T.astype(BF16)


def _dsa_mixer(k, vt, kidx, qt, qit, azt, wt, tz, topk):
    bsz, seq, width = k.shape
    n_heads = width // HEAD_DIM
    t = ATT_TILE
    tq = min(ATT_QUERIES, seq)
    per_b = lambda shape: pl.BlockSpec((1,) + shape, lambda b, q: (b, 0, 0))
    qtile = lambda r: pl.BlockSpec((1, r, tq), lambda b, q: (b, 0, q))
    kern = functools.partial(_dsa_kernel, topk=topk, n_heads=n_heads)
    vrows = vt.shape[1]
    return pl.pallas_call(
        kern,
        out_shape=jax.ShapeDtypeStruct((bsz, seq, width), BF16),
        grid=(bsz, seq // tq),
        in_specs=[per_b((seq, width)), per_b((vrows, seq)), per_b((seq, LANES)),
                  qtile(width), qtile(width), qtile(width), qtile(IDX_HEADS),
                  pl.BlockSpec((3, n_heads, t, t), lambda b, q: (0, 0, 0, 0))],
        out_specs=pl.BlockSpec((1, tq, width), lambda b, q: (b, q, 0)),
        scratch_shapes=[pltpu.VMEM((seq, tq), F32),
                        pltpu.VMEM((IDX_HEADS, 2 * IDX_DIM, tq), BF16),
                        pltpu.VMEM((n_heads, 2 * HEAD_DIM, tq), BF16),
                        pltpu.VMEM((n_heads, tq), F32),
                        pltpu.VMEM((vrows, tq), F32),
                        pltpu.VMEM((seq, tq), BF16),
                        pltpu.VMEM((n_heads, t, tq), BF16), pltpu.VMEM((n_heads, t, tq), BF16),
                        pltpu.VMEM((n_heads, tq), F32), pltpu.VMEM((n_heads, tq), F32),
                        pltpu.VMEM((n_heads, t, tq), BF16), pltpu.VMEM((n_heads, t, tq), BF16),
                        pltpu.VMEM((n_heads, tq), F32), pltpu.VMEM((n_heads, tq), F32),
                        pltpu.VMEM((8, tq), F32)],
        compiler_params=_cparams(("parallel", "arbitrary")),
        name="dsa_mixer",
    )(k, vt, kidx, qt, qit, azt, wt, tz)


def kernel(x, c, rel_bias, norm_g, w_ada, b_ada, w_in, q_gain, k_gain, a_re, a_im, log_dt,
           b_re, b_im, c_re, c_im, d_skip, w_glu, b_glu, w_out):
    bsz, seq, d = x.shape
    depth = w_in.shape[0]
    width = d // 2
    n_heads = width // HEAD_DIM
    topk = min(TOPK_MAX, seq // 4)
    assert seq % min(ATT_QUERIES, seq) == 0 and ATT_QUERIES % ATT_TILE == 0
    assert seq % ATT_TILE == 0 and seq % SSM_CHUNK == 0
    assert ATT_TILE + 1 >= MAX_DISTANCE
    tz = _bias_prep(rel_bias, n_heads)
    for l in range(depth):
        mod = _adaln_mod(c, w_ada[l], b_ada[l])
        u, zs, k, kidx, qt, vt, qit, azt, wt = _in_proj(x, mod, norm_g[l], w_in[l],
                                                        q_gain[l], k_gain[l])
        tabs = _ssm_prep(a_re[l], a_im[l], log_dt[l], b_re[l], b_im[l])
        y_attn = _dsa_mixer(k, vt, kidx, qt, qit, azt, wt, tz, topk)
        x = _s5_mixer_out(u, zs, tabs, c_re[l], c_im[l], d_skip[l], w_glu[l], b_glu[l],
                          x, y_attn, w_out[l], mod)
    return x
```

```python
import functools
import math

import jax
import jax.numpy as jnp
import numpy as np
from jax import lax
from jax.experimental import pallas as pl
from jax.experimental.pallas import tpu as pltpu

F32 = jnp.float32
BF16 = jnp.bfloat16

SSM_GROUP = 16
SSM_STATE = 64
HEAD_DIM = 64
VAUG = 16
IDX_HEADS = 8
IDX_DIM = 64
TOPK_MAX = 256
NUM_BUCKETS = 32
MAX_DISTANCE = 128
EPS = 1e-6

LANES = 128
V7X_VMEM_LIMIT_BYTES = 56 * 1024 * 1024

SSM_CHUNK = 128
SSM_TILE = 512
SSM_SUB = 256
SSM_BLOCK_CH = 128
PROJ_TILE = 512
ATT_TILE = 256
ATT_QUERIES = 512
BISECT_MIN_ITERS = 14
BISECT_MAX_ITERS = 26
NEG = -(2.0 ** 100)
LOG2E = math.log2(math.e)

NT_DIMS = (((1,), (1,)), ((), ()))


def _cparams(sem):
    return pltpu.CompilerParams(dimension_semantics=sem,
                                vmem_limit_bytes=V7X_VMEM_LIMIT_BYTES)


def _sigmoid(x):
    return 1.0 / (1.0 + jnp.exp(-x))


def _silu(x):
    return x * _sigmoid(x)


def _ssm_prep_kernel(are_r, aim_r, ldt_r, are_c, aim_c, ldt_c, bre, bim,
                     ep_re, ep_im, em_re, em_im, laml_re, laml_im, bb_re, bb_im):
    dt = jnp.exp(ldt_r[...])
    ar = are_r[...] * dt
    ai = aim_r[...] * dt
    n = ar.shape[-1]
    tau = lax.broadcasted_iota(jnp.int32, (SSM_CHUNK, n), 0).astype(F32)
    mag_p = jnp.exp(tau * ar)
    mag_m = jnp.exp(-tau * ar)
    cs = jnp.cos(tau * ai)
    sn = jnp.sin(tau * ai)
    ep_re[...] = mag_p * cs
    ep_im[...] = mag_p * sn
    em_re[...] = mag_m * cs
    em_im[...] = -(mag_m * sn)
    mag_l = jnp.exp(float(SSM_CHUNK) * ar)
    laml_re[...] = mag_l * jnp.cos(float(SSM_CHUNK) * ai)
    laml_im[...] = mag_l * jnp.sin(float(SSM_CHUNK) * ai)
    a = are_c[...]
    b = aim_c[...]
    dtc = jnp.exp(ldt_c[...])
    mag = jnp.exp(a * dtc)
    x = mag * jnp.cos(b * dtc) - 1.0
    y = mag * jnp.sin(b * dtc)
    den = a * a + b * b
    cre = (x * a + y * b) / den
    cim = (y * a - x * b) / den
    bb_re[...] = cre * bre[...] - cim * bim[...]
    bb_im[...] = cre * bim[...] + cim * bre[...]


def _ssm_prep(a_re, a_im, log_dt, b_re, b_im):
    g, p = a_re.shape
    n = g * p
    ldt = jnp.broadcast_to(log_dt[:, None], (g, p))
    row = lambda t: t.reshape(1, n)
    col = lambda t: t.reshape(n, 1)
    tab = jax.ShapeDtypeStruct((SSM_CHUNK, n), F32)
    vec = jax.ShapeDtypeStruct((1, n), F32)
    bsh = jax.ShapeDtypeStruct((n, SSM_GROUP), F32)
    return pl.pallas_call(
        _ssm_prep_kernel,
        out_shape=(tab, tab, tab, tab, vec, vec, bsh, bsh),
        name="ssm_prep",
    )(row(a_re), row(a_im), row(ldt), col(a_re), col(a_im), col(ldt),
      b_re.reshape(n, SSM_GROUP), b_im.reshape(n, SSM_GROUP))


def _bias_prep_kernel(rb_ref, o_ref):
    t = o_ref.shape[-1]
    n_heads = o_ref.shape[1]
    i = lax.broadcasted_iota(jnp.int32, (t, t), 0)
    j = lax.broadcasted_iota(jnp.int32, (t, t), 1)
    max_exact = NUM_BUCKETS // 2

    def bucket_of(idx):
        dist = jnp.maximum(idx * t + j - i, 0)
        d = jnp.maximum(dist, 1).astype(F32)
        large = max_exact + (jnp.log(d / max_exact) / math.log(MAX_DISTANCE / max_exact)
                             * (NUM_BUCKETS - max_exact)).astype(jnp.int32)
        large = jnp.minimum(large, NUM_BUCKETS - 1)
        return jnp.where(dist < max_exact, dist, large)

    def bias(bucket, h):
        acc = jnp.zeros((t, t), F32)
        for b in range(NUM_BUCKETS):
            acc = jnp.where(bucket == b, rb_ref[b, h], acc)
        return acc

    buckets = [bucket_of(idx) for idx in range(3)]
    for h in range(n_heads):
        far = bias(buckets[2], h)
        for idx in range(3):
            o_ref[idx, h] = ((bias(buckets[idx], h) - far) * LOG2E).astype(o_ref.dtype)


def _bias_prep(rel_bias, n_heads):
    t = ATT_TILE
    return pl.pallas_call(
        _bias_prep_kernel,
        out_shape=jax.ShapeDtypeStruct((3, n_heads, t, t), BF16),
        in_specs=[pl.BlockSpec(memory_space=pltpu.SMEM)],
        name="bias_prep",
    )(rel_bias)


def _mod_kernel(c_ref, w_ref, b_ref, o_ref):
    c = c_ref[...]
    cond = _silu(c)
    o_ref[...] = jnp.dot(cond.astype(BF16), w_ref[...].astype(BF16),
                         preferred_element_type=F32) + b_ref[...]


def _adaln_mod(c, w_ada, b_ada):
    bsz, d = c.shape
    n = w_ada.shape[1]
    tn = 512
    return pl.pallas_call(
        _mod_kernel,
        out_shape=jax.ShapeDtypeStruct((bsz, n), F32),
        grid=(n // tn,),
        in_specs=[pl.BlockSpec((bsz, d), lambda i: (0, 0)),
                  pl.BlockSpec((d, tn), lambda i: (0, i)),
                  pl.BlockSpec((1, tn), lambda i: (0, i))],
        out_specs=pl.BlockSpec((bsz, tn), lambda i: (0, i)),
        compiler_params=_cparams(("arbitrary",)),
        name="adaln_mod",
    )(c, w_ada, b_ada.reshape(1, n))


def _in_proj_kernel(x_ref, mod_ref, g_ref, wnat_ref, wt_ref, ww_ref, mblk_ref,
                    kg_ref, qg_ref,
                    u_ref, zs_ref, k_ref, kidx_ref, qt_ref, vt_ref, qit_ref, azt_ref, wt_out_ref,
                    *, d_model, width, n_heads):
    x = x_ref[0]
    shift = mod_ref[0, :, 0:d_model]
    scale = mod_ref[0, :, d_model:2 * d_model]
    ms = jnp.mean(x * x, axis=-1, keepdims=True)
    xn = x * lax.rsqrt(ms + EPS) * g_ref[...]
    h = (xn * (1.0 + scale) + shift).astype(BF16)
    tm = h.shape[0]

    nat = jnp.dot(h, wnat_ref[...], preferred_element_type=F32)
    u_ref[0] = nat[:, 0:width].astype(BF16)
    zs_ref[0] = nat[:, width:2 * width].astype(BF16)
    k = nat[:, 2 * width:3 * width]
    kms = jnp.dot((k * k).astype(BF16), mblk_ref[...], preferred_element_type=F32)
    k_ref[0] = (k * lax.rsqrt(kms + EPS) * kg_ref[...]).astype(BF16)
    kidx_ref[0] = nat[:, 3 * width:3 * width + LANES].astype(BF16)

    tr = lax.dot_general(wt_ref[...], h, NT_DIMS, preferred_element_type=F32)
    q3 = tr[0:width].reshape(n_heads, HEAD_DIM, tm)
    qms = jnp.mean(q3 * q3, axis=1, keepdims=True)
    qn = q3 * lax.rsqrt(qms + EPS) * qg_ref[...][None] * (HEAD_DIM ** -0.5 * LOG2E)
    qt_ref[0] = qn.reshape(width, tm).astype(BF16)
    v3 = tr[width:2 * width].astype(BF16).reshape(n_heads, HEAD_DIM, tm)
    aug = lax.broadcasted_iota(jnp.int32, (n_heads, VAUG, tm), 1)
    aug = jnp.where(aug == 0, 1.0, 0.0).astype(BF16)
    vt_ref[0] = jnp.concatenate([v3, aug], axis=1).reshape(n_heads * (HEAD_DIM + VAUG), tm)
    qit_ref[0] = tr[2 * width:3 * width].astype(BF16)
    azt_ref[0] = tr[3 * width:4 * width].astype(BF16)
    wt_out_ref[0] = lax.dot_general(ww_ref[...], h, NT_DIMS,
                                    preferred_element_type=F32) * (IDX_HEADS ** -0.5)


def _in_proj(x, mod, norm_g, w_in, q_gain, k_gain):
    bsz, seq, d = x.shape
    width = d // 2
    n_heads = width // HEAD_DIM
    tm = min(PROJ_TILE, seq)
    o = np.cumsum([0, width, width, width, width, width, width,
                   IDX_HEADS * IDX_DIM, IDX_DIM, IDX_HEADS])
    col = lambda i: w_in[:, int(o[i]):int(o[i + 1])]
    w_kidx = jnp.pad(col(7), ((0, 0), (0, LANES - IDX_DIM)))
    w_nat = jnp.concatenate([col(0), col(1), col(3), w_kidx], axis=1).astype(BF16)
    w_t = jnp.concatenate([col(2), col(4), col(6), col(5)], axis=1).T.astype(BF16)
    w_w = col(8).T.astype(BF16)
    hid = np.arange(width) // HEAD_DIM
    mblk = jnp.asarray((hid[:, None] == hid[None, :]).astype(np.float32) / HEAD_DIM, BF16)
    kg = jnp.tile(k_gain, n_heads).reshape(1, width)
    qg = q_gain.reshape(HEAD_DIM, 1)

    nat_w = w_nat.shape[1]
    bs = jax.ShapeDtypeStruct
    const = lambda shape: pl.BlockSpec(shape, lambda b, s: (0,) * len(shape))
    tok = lambda w: pl.BlockSpec((1, tm, w), lambda b, s: (b, s, 0))
    trn = lambda r: pl.BlockSpec((1, r, tm), lambda b, s: (b, 0, s))
    kern = functools.partial(_in_proj_kernel, d_model=d, width=width, n_heads=n_heads)
    vrows = n_heads * (HEAD_DIM + VAUG)
    return pl.pallas_call(
        kern,
        out_shape=(bs((bsz, seq, width), BF16), bs((bsz, seq, width), BF16),
                   bs((bsz, seq, width), BF16), bs((bsz, seq, LANES), BF16),
                   bs((bsz, width, seq), BF16), bs((bsz, vrows, seq), BF16),
                   bs((bsz, width, seq), BF16), bs((bsz, width, seq), BF16),
                   bs((bsz, IDX_HEADS, seq), F32)),
        grid=(bsz, seq // tm),
        in_specs=[tok(d),
                  pl.BlockSpec((1, 1, 3 * d), lambda b, s: (b, 0, 0)),
                  const((1, d)), const((d, nat_w)), const((4 * width, d)),
                  const((IDX_HEADS, d)), const((width, width)),
                  const((1, width)), const((HEAD_DIM, 1))],
        out_specs=(tok(width), tok(width), tok(width), tok(LANES),
                   trn(width), trn(vrows), trn(width), trn(width), trn(IDX_HEADS)),
        compiler_params=_cparams(("parallel", "arbitrary")),
        name="in_proj",
    )(x, mod.reshape(bsz, 1, 3 * d), norm_g.reshape(1, d), w_nat, w_t, w_w, mblk, kg, qg)


def _s5_kernel(u_ref, zs_ref, bblk_ref, cre_ref, cim_ref, ep_re, ep_im, em_re, em_im,
               laml_re, laml_im, tri_ref, dsk_ref, wglu_ref, bglu_ref,
               xres_ref, ya_ref, w1_ref, w2_ref, mod_ref,
               o_ref, car_re, car_im, w_ref, x_ref, *, n_blocks, d_model):
    tm = u_ref.shape[1]
    sub = tri_ref.shape[0]
    nch = sub // SSM_CHUNK
    nst = cre_ref.shape[1]

    @pl.when(pl.program_id(1) == 0)
    def _():
        car_re[...] = jnp.zeros_like(car_re)
        car_im[...] = jnp.zeros_like(car_im)

    tri = tri_ref[...]
    halves = [slice(r * sub, (r + 1) * sub) for r in range(tm // sub)]
    for rs in halves:
        for j in range(n_blocks):
            st = slice(j * nst, (j + 1) * nst)
            ch = slice(j * SSM_BLOCK_CH, (j + 1) * SSM_BLOCK_CH)
            p = jnp.dot(u_ref[0, rs, ch], bblk_ref[j], preferred_element_type=F32)
            p_re = p[:, :nst].astype(BF16).reshape(nch, SSM_CHUNK, nst)
            p_im = p[:, nst:].astype(BF16).reshape(nch, SSM_CHUNK, nst)
            emr = em_re[:, st][None]
            emi = em_im[:, st][None]
            w_ref[j, rs, 0:nst] = (p_re * emr - p_im * emi).reshape(sub, nst)
            w_ref[j, rs, nst:2 * nst] = (p_re * emi + p_im * emr).reshape(sub, nst)
    for rs in halves:
        for j in range(n_blocks):
            st = slice(j * nst, (j + 1) * nst)
            cum = jnp.dot(tri, w_ref[j, rs, :], preferred_element_type=F32)
            cum_re = cum[:, :nst].reshape(nch, SSM_CHUNK, nst)
            cum_im = cum[:, nst:].reshape(nch, SSM_CHUNK, nst)
            lr = laml_re[:, st]
            li = laml_im[:, st]
            cr = car_re[:, st]
            ci = car_im[:, st]
            crs, cis = [], []
            for c in range(nch):
                crs.append(cr)
                cis.append(ci)
                tr_ = cum_re[c, SSM_CHUNK - 1:SSM_CHUNK, :] + cr
                ti_ = cum_im[c, SSM_CHUNK - 1:SSM_CHUNK, :] + ci
                cr = lr * tr_ - li * ti_
                ci = lr * ti_ + li * tr_
            car_re[:, st] = cr
            car_im[:, st] = ci
            a_re = (cum_re + jnp.stack(crs, axis=0)).astype(BF16)
            a_im = (cum_im + jnp.stack(cis, axis=0)).astype(BF16)
            epr = ep_re[:, st][None]
            epi = ep_im[:, st][None]
            x_ref[j, rs, 0:nst] = (a_re * epr - a_im * epi).reshape(sub, nst)
            x_ref[j, rs, nst:2 * nst] = (a_re * epi + a_im * epr).reshape(sub, nst)
    ys = []
    for j in range(n_blocks):
        ys.append(jnp.dot(x_ref[j, :, 0:nst], cre_ref[j], preferred_element_type=F32)
                  - jnp.dot(x_ref[j, :, nst:2 * nst], cim_ref[j], preferred_element_type=F32))
    y = jnp.concatenate(ys, axis=1) + dsk_ref[...] * u_ref[0].astype(F32)
    z = 0.5 * y * (1.0 + jnp.tanh(math.sqrt(2.0 / math.pi) * (y + 0.044715 * (y * y * y))))
    gl = jnp.dot(z.astype(BF16), wglu_ref[...], preferred_element_type=F32) + bglu_ref[...]
    zz = z * _sigmoid(gl)
    y_ssm = (zz * _silu(zs_ref[0].astype(F32))).astype(BF16)
    gate = mod_ref[0, :, 2 * d_model:3 * d_model]
    yo = (jnp.dot(y_ssm, w1_ref[...], preferred_element_type=F32)
          + jnp.dot(ya_ref[0], w2_ref[...], preferred_element_type=F32))
    o_ref[0] = xres_ref[0] + gate * yo


def _s5_mixer_out(u, zs, tabs, c_re, c_im, d_skip, w_glu, b_glu, x, y_attn, w_out, mod):
    bsz, seq, width = u.shape
    d = x.shape[-1]
    ep_re, ep_im, em_re, em_im, laml_re, laml_im, bb_re, bb_im = tabs
    tm = min(SSM_TILE, seq)
    gpb = SSM_BLOCK_CH // SSM_GROUP
    n_blocks = width // SSM_BLOCK_CH
    nst = gpb * SSM_STATE
    eye = jnp.eye(gpb, dtype=F32)

    def b_block(t):
        t4 = t.reshape(n_blocks, gpb, SSM_STATE, SSM_GROUP)
        return jnp.einsum('jgpc,gh->jgchp', t4, eye).reshape(n_blocks, SSM_BLOCK_CH, nst)

    def c_block(t):
        t4 = t.reshape(n_blocks, gpb, SSM_GROUP, SSM_STATE)
        return jnp.einsum('jgcp,gh->jhpgc', t4, eye).reshape(n_blocks, nst, SSM_BLOCK_CH)

    bblk = jnp.concatenate([b_block(bb_re), b_block(bb_im)], axis=2).astype(BF16)
    cre = c_block(c_re).astype(BF16)
    cim = c_block(c_im).astype(BF16)
    sub = min(SSM_SUB, tm)
    t_idx = np.arange(sub)
    tri = jnp.asarray(((t_idx[:, None] >= t_idx[None, :])
                       & (t_idx[:, None] // SSM_CHUNK == t_idx[None, :] // SSM_CHUNK)
                       ).astype(np.float32), BF16)
    n = ep_re.shape[1]
    const = lambda shape: pl.BlockSpec(shape, lambda b, s: (0,) * len(shape))
    tok = pl.BlockSpec((1, tm, width), lambda b, s: (b, s, 0))
    tokd = pl.BlockSpec((1, tm, d), lambda b, s: (b, s, 0))
    kern = functools.partial(_s5_kernel, n_blocks=n_blocks, d_model=d)
    return pl.pallas_call(
        kern,
        out_shape=jax.ShapeDtypeStruct((bsz, seq, d), x.dtype),
        grid=(bsz, seq // tm),
        in_specs=[tok, tok,
                  const((n_blocks, SSM_BLOCK_CH, 2 * nst)),
                  const((n_blocks, nst, SSM_BLOCK_CH)), const((n_blocks, nst, SSM_BLOCK_CH)),
                  const((SSM_CHUNK, n)), const((SSM_CHUNK, n)),
                  const((SSM_CHUNK, n)), const((SSM_CHUNK, n)),
                  const((1, n)), const((1, n)), const((sub, sub)),
                  const((1, width)), const((width, width)), const((1, width)),
                  tokd, tok, const((width, d)), const((width, d)),
                  pl.BlockSpec((1, 1, 3 * d), lambda b, s: (b, 0, 0))],
        out_specs=tokd,
        scratch_shapes=[pltpu.VMEM((1, n), F32), pltpu.VMEM((1, n), F32),
                        pltpu.VMEM((n_blocks, tm, 2 * nst), BF16),
                        pltpu.VMEM((n_blocks, tm, 2 * nst), BF16)],
        compiler_params=_cparams(("parallel", "arbitrary")),
        name="s5_mixer_out",
    )(u, zs, bblk, cre, cim, ep_re.astype(BF16), ep_im.astype(BF16),
      em_re.astype(BF16), em_im.astype(BF16), laml_re, laml_im, tri,
      d_skip.reshape(1, width), w_glu.astype(BF16), b_glu.reshape(1, width),
      x, y_attn, w_out[:width].astype(BF16), w_out[width:].astype(BF16),
      mod.reshape(bsz, 1, 3 * d))


def _dsa_kernel(k_ref, vt_ref, kidx_ref, qt_ref, qit_ref, azt_ref, wt_ref, tz_ref,
                o_ref, sc_ref, qip_ref, qp_ref, m_ref, acc_ref, mask_ref, s_ref, s2_ref,
                mt_ref, mt2_ref,
                p_ref, p2_ref, c_ref, c2_ref, sel_ref,
                *, topk, n_heads):
    t = ATT_TILE
    tq = qt_ref.shape[2]
    kpq = tq // t
    qi = pl.program_id(1)
    n_far = qi * kpq
    nk = n_far + kpq
    seq = k_ref.shape[1]
    row = lax.broadcasted_iota(jnp.int32, (t, tq), 0)
    colq = lax.broadcasted_iota(jnp.int32, (t, tq), 1)
    lane = lax.broadcasted_iota(jnp.int32, (1, tq), 1)
    pos = qi * tq + lane
    kq = jnp.minimum(pos + 1, topk).astype(F32)

    def tile_rows(kt):
        return pl.ds(pl.multiple_of(kt * t, t), t)

    zpad = jnp.zeros((HEAD_DIM, tq), BF16)
    for h in range(IDX_HEADS):
        qip_ref[h] = jnp.concatenate([qit_ref[0, h * IDX_DIM:(h + 1) * IDX_DIM, :], zpad], axis=0)
    for h in range(n_heads):
        qh = qt_ref[0, h * HEAD_DIM:(h + 1) * HEAD_DIM, :]
        qp_ref[h] = jnp.concatenate([qh, zpad] if h % 2 == 0 else [zpad, qh], axis=0)

    def score_tile(kt, lane0=0):
        kx = kidx_ref[0, tile_rows(kt), :]
        s = jnp.zeros((t, tq - lane0), F32)
        for h in range(IDX_HEADS):
            rel = jnp.dot(kx, qip_ref[h, :, lane0:], preferred_element_type=F32)
            s = s + wt_ref[0, h:h + 1, lane0:] * jnp.maximum(rel, 0.0)
        return s

    def score_far(kt, carry):
        smin, smax = carry
        s = score_tile(kt)
        sc_ref[tile_rows(kt), :] = s
        return (jnp.minimum(smin, jnp.min(s, axis=0, keepdims=True)),
                jnp.maximum(smax, jnp.max(s, axis=0, keepdims=True)))

    smin, smax = lax.fori_loop(
        0, n_far, score_far,
        (jnp.full((1, tq), jnp.inf, F32), jnp.full((1, tq), -jnp.inf, F32)))
    for d in range(kpq):
        l0 = d * t
        s = score_tile(n_far + d, l0)
        causal = (lax.broadcasted_iota(jnp.int32, (t, tq - l0), 0)
                  <= lax.broadcasted_iota(jnp.int32, (t, tq - l0), 1))
        rows_d = tile_rows(n_far + d)
        if l0:
            sc_ref[rows_d, :l0] = jnp.full((t, l0), -jnp.inf, F32)
        sc_ref[rows_d, l0:] = jnp.where(causal, s, -jnp.inf)
        lo_d = jnp.min(jnp.where(causal, s, jnp.inf), axis=0, keepdims=True)
        hi_d = jnp.max(jnp.where(causal, s, -jnp.inf), axis=0, keepdims=True)
        if l0:
            lo_d = jnp.concatenate([jnp.full((1, l0), jnp.inf, F32), lo_d], axis=1)
            hi_d = jnp.concatenate([jnp.full((1, l0), -jnp.inf, F32), hi_d], axis=1)
        smin = jnp.minimum(smin, lo_d)
        smax = jnp.maximum(smax, hi_d)

    def count_where(pred_fn):
        def tile_count(kt, lane0):
            m = pred_fn(sc_ref[tile_rows(kt), lane0:], lane0)
            return jnp.sum(jnp.where(m, 1.0, 0.0).reshape(4, t // 32, 8, tq - lane0), axis=1)

        def body(j, acc):
            for d in range(kpq):
                acc = acc + tile_count(j * kpq + d, 0)
            return acc
        acc = lax.fori_loop(0, qi, body, jnp.zeros((4, 8, tq), F32))
        total = jnp.sum((acc + tile_count(n_far, 0)).reshape(32, tq), axis=0, keepdims=True)
        for d in range(1, kpq):
            part = jnp.sum(tile_count(n_far + d, d * t).reshape(32, tq - d * t),
                           axis=0, keepdims=True)
            total = total + jnp.concatenate([jnp.zeros((1, d * t), F32), part], axis=1)
        return total

    def lanes_from(thr, lane0):
        if isinstance(thr, float) or lane0 == 0:
            return thr
        sel_ref[4:5, :] = thr
        return sel_ref[4:5, lane0:]

    def count_ge(thr):
        return count_where(lambda s_, l0: s_ >= lanes_from(thr, l0))

    def count_gt(thr):
        return count_where(lambda s_, l0: s_ > lanes_from(thr, l0))

    def min_where(pred_fn):
        def body(kt, acc):
            s = sc_ref[tile_rows(kt), :]
            return jnp.minimum(acc, jnp.min(jnp.where(pred_fn(s), s, jnp.inf), axis=0, keepdims=True))
        return lax.fori_loop(0, nk, body, jnp.full((1, tq), jnp.inf, F32))

    def bisect(st):
        lo, hi, c_lo = st
        mid = lo + 0.5 * (hi - lo)
        cnt = count_ge(mid)
        up = cnt >= kq
        return jnp.where(up, mid, lo), jnp.where(up, hi, mid), jnp.where(up, cnt, c_lo)

    c_ge0 = count_ge(0.0)
    c_gt0 = count_gt(0.0)
    above = c_gt0 >= kq
    below = c_ge0 < kq
    at_zero = jnp.logical_not(above | below)
    zero_tie = at_zero & (c_ge0 > kq)
    lo_neg = above & (smin <= 0.0)
    lo0 = jnp.where(at_zero | lo_neg, 0.0, smin)
    c_lo0 = jnp.where(at_zero | lo_neg, c_ge0, (pos + 1).astype(F32))
    hi0 = jnp.where(at_zero, 0.0, jnp.where(below, jnp.minimum(smax, 0.0), smax))
    open_f = jnp.where(zero_tie, 0.0, 1.0)

    st = lax.fori_loop(0, BISECT_MIN_ITERS, lambda _, s_: bisect(s_), (lo0, hi0, c_lo0))

    def more_cond(c):
        return (c[3] > 0.0) & (c[4] < BISECT_MAX_ITERS)

    def more_body(c):
        st_ = (c[0], c[1], c[2])
        flag = jnp.max((c[2] - kq) * open_f)
        return bisect(st_) + (flag, c[4] + 1)

    lo, _, c_lo, _, _ = lax.while_loop(
        more_cond, more_body, st + (jnp.float32(1.0), jnp.int32(BISECT_MIN_ITERS)))

    sel_ref[0:1, :] = lo
    sel_ref[1:2, :] = jnp.zeros((1, tq), F32)
    sel_ref[2:3, :] = kq - c_gt0
    sel_ref[3:4, :] = jnp.where(zero_tie, 1.0, 0.0)

    @pl.when(jnp.max((c_lo - kq) * open_f) > 0.0)
    def _():
        def peel_cond(c):
            return c[6] > 0.0

        def peel(c):
            lo_, c_lo_, taup_, ngt_, gtie_ = c[0], c[1], c[2], c[3], c[4]
            taup_n = min_where(lambda s_: s_ >= lo_)
            ngt_n = count_gt(taup_n)
            active = (c_lo_ > kq) & (open_f > 0.0) & (gtie_ == 0.0)
            is_tie = active & (ngt_n < kq)
            need_peel = active & (ngt_n >= kq)
            nxt = min_where(lambda s_: s_ > taup_n)
            return (jnp.where(need_peel, nxt, lo_), jnp.where(need_peel, ngt_n, c_lo_),
                    jnp.where(is_tie, taup_n, taup_), jnp.where(is_tie, ngt_n, ngt_),
                    jnp.where(is_tie, 1.0, gtie_), c[5],
                    jnp.max(jnp.where(need_peel, 1.0, 0.0)))

        zero = jnp.zeros((1, tq), F32)
        lo2, _, taup2, ngt2, gtie2, _, _ = lax.while_loop(
            peel_cond, peel, (lo, c_lo, zero, zero, zero, zero, jnp.float32(1.0)))
        gt = gtie2 > 0.0
        sel_ref[0:1, :] = lo2
        sel_ref[1:2, :] = jnp.where(gt, taup2, sel_ref[1:2, :])
        sel_ref[2:3, :] = jnp.where(gt, kq - ngt2, sel_ref[2:3, :])
        sel_ref[3:4, :] = jnp.where(gt, 1.0, sel_ref[3:4, :])

    lo_f = sel_ref[0:1, :]
    tie_f = sel_ref[3:4, :]
    any_tie = jnp.max(tie_f)

    @pl.when(any_tie <= 0.0)
    def _():
        def body(kt, c):
            s_ = sc_ref[tile_rows(kt), :]
            mask_ref[tile_rows(kt), :] = jnp.where(s_ >= lo_f, 0.0, NEG).astype(BF16)
            return c
        lax.fori_loop(0, nk, body, 0)

    @pl.when(any_tie > 0.0)
    def _():
        taup = sel_ref[1:2, :]
        need = jnp.where(tie_f > 0.0, sel_ref[2:3, :], float(2 * seq))
        tr_ = lax.broadcasted_iota(jnp.int32, (t, t), 0)
        tc_ = lax.broadcasted_iota(jnp.int32, (t, t), 1)
        tri = jnp.where(tc_ <= tr_, 1.0, 0.0).astype(BF16)

        def body(kt, seen):
            s_ = sc_ref[tile_rows(kt), :]
            z = s_ == taup
            pref = jnp.dot(tri, jnp.where(z, 1.0, 0.0).astype(BF16), preferred_element_type=F32)
            keep = (s_ >= lo_f) & jnp.logical_not(z & (seen + pref > need))
            mask_ref[tile_rows(kt), :] = jnp.where(keep, 0.0, NEG).astype(BF16)
            return seen + pref[t - 1:t, :]
        lax.fori_loop(0, nk, body, jnp.zeros((1, tq), F32))

    va = HEAD_DIM + VAUG

    def logits_head(h, kt, s_buf, mt_buf):
        rows = tile_rows(kt)
        pair = slice((h // 2) * 2 * HEAD_DIM, (h // 2 + 1) * 2 * HEAD_DIM)
        s = jnp.dot(k_ref[0, rows, pair], qp_ref[h], preferred_element_type=F32)
        s = s.astype(BF16) + mask_ref[rows, :]
        parts = []
        for g in range(tq // t):
            near = jnp.clip(n_far + g - kt, 0, 2)
            parts.append(s[:, g * t:(g + 1) * t] + tz_ref[near, h])
        s = jnp.concatenate(parts, axis=1) if len(parts) > 1 else parts[0]
        s_buf[h] = s
        mt_buf[h:h + 1, :] = jnp.max(s, axis=0, keepdims=True).astype(F32)

    def softmax_head(h, s_buf, mt_buf, p_buf, corr_buf):
        m_old = m_ref[h:h + 1, :]
        m_new = jnp.maximum(m_old, mt_buf[h:h + 1, :])
        corr_buf[h:h + 1, :] = jnp.exp2(m_old - m_new)
        p_buf[h] = jnp.exp2(s_buf[h] - m_new.astype(BF16))
        m_ref[h:h + 1, :] = m_new

    def pv_head(h, kt, p_buf, corr_buf):
        rows = tile_rows(kt)
        hs = slice(h * va, (h + 1) * va)
        acc_ref[hs, :] = acc_ref[hs, :] * corr_buf[h:h + 1, :] + jnp.dot(
            vt_ref[0, hs, rows], p_buf[h], preferred_element_type=F32)

    def logits_stage(kt, s_buf, mt_buf):
        for h in range(n_heads):
            logits_head(h, kt, s_buf, mt_buf)

    def pv_stage(kt, p_buf, corr_buf):
        for h in range(n_heads):
            pv_head(h, kt, p_buf, corr_buf)

    def trip(kt_pv, p_old, c_old, kt_next, s_next, mt_next, s_cur, mt_cur, p_cur, c_cur):
        for h in range(n_heads):
            pv_head(h, kt_pv, p_old, c_old)
            logits_head(h, kt_next, s_next, mt_next)
            softmax_head(h, s_cur, mt_cur, p_cur, c_cur)

    m_ref[...] = jnp.full_like(m_ref, NEG)
    acc_ref[...] = jnp.zeros_like(acc_ref)
    p2_ref[...] = jnp.zeros_like(p2_ref)
    c2_ref[...] = jnp.ones_like(c2_ref)
    logits_stage(0, s_ref, mt_ref)
    last = nk - 1

    def kv_body(j, c):
        i = 2 * j
        trip(jnp.maximum(i - 1, 0), p2_ref, c2_ref,
             jnp.minimum(i + 1, last), s2_ref, mt2_ref,
             s_ref, mt_ref, p_ref, c_ref)

        @pl.when(i + 1 < nk)
        def _():
            trip(i, p_ref, c_ref,
                 jnp.minimum(i + 2, last), s_ref, mt_ref,
                 s2_ref, mt2_ref, p2_ref, c2_ref)
        return c

    lax.fori_loop(0, (nk + 1) // 2, kv_body, 0)

    if kpq % 2 == 0:
        pv_stage(last, p2_ref, c2_ref)
    else:
        @pl.when(jnp.bitwise_and(nk, 1) == 1)
        def _():
            pv_stage(last, p_ref, c_ref)

        @pl.when(jnp.bitwise_and(nk, 1) == 0)
        def _():
            pv_stage(last, p2_ref, c2_ref)

    acc = acc_ref[...].reshape(n_heads, va, tq)
    out = acc[:, :HEAD_DIM, :] * (1.0 / acc[:, HEAD_DIM:HEAD_DIM + 1, :])
    out = out.reshape(n_heads * HEAD_DIM, tq) * _silu(azt_ref[0].astype(F32))
    o_ref[0] = out.T.astype(BF16)


def _dsa_mixer(k, vt, kidx, qt, qit, azt, wt, tz, topk):
    bsz, seq, width = k.shape
    n_heads = width // HEAD_DIM
    t = ATT_TILE
    tq = min(ATT_QUERIES, seq)
    per_b = lambda shape: pl.BlockSpec((1,) + shape, lambda b, q: (b, 0, 0))
    qtile = lambda r: pl.BlockSpec((1, r, tq), lambda b, q: (b, 0, q))
    kern = functools.partial(_dsa_kernel, topk=topk, n_heads=n_heads)
    vrows = vt.shape[1]
    return pl.pallas_call(
        kern,
        out_shape=jax.ShapeDtypeStruct((bsz, seq, width), BF16),
        grid=(bsz, seq // tq),
        in_specs=[per_b((seq, width)), per_b((vrows, seq)), per_b((seq, LANES)),
                  qtile(width), qtile(width), qtile(width), qtile(IDX_HEADS),
                  pl.BlockSpec((3, n_heads, t, t), lambda b, q: (0, 0, 0, 0))],
        out_specs=pl.BlockSpec((1, tq, width), lambda b, q: (b, q, 0)),
        scratch_shapes=[pltpu.VMEM((seq, tq), F32),
                        pltpu.VMEM((IDX_HEADS, 2 * IDX_DIM, tq), BF16),
                        pltpu.VMEM((n_heads, 2 * HEAD_DIM, tq), BF16),
                        pltpu.VMEM((n_heads, tq), F32),
                        pltpu.VMEM((vrows, tq), F32),
                        pltpu.VMEM((seq, tq), BF16),
                        pltpu.VMEM((n_heads, t, tq), BF16), pltpu.VMEM((n_heads, t, tq), BF16),
                        pltpu.VMEM((n_heads, tq), F32), pltpu.VMEM((n_heads, tq), F32),
                        pltpu.VMEM((n_heads, t, tq), BF16), pltpu.VMEM((n_heads, t, tq), BF16),
                        pltpu.VMEM((n_heads, tq), F32), pltpu.VMEM((n_heads, tq), F32),
                        pltpu.VMEM((8, tq), F32)],
        compiler_params=_cparams(("parallel", "arbitrary")),
        name="dsa_mixer",
    )(k, vt, kidx, qt, qit, azt, wt, tz)


def kernel(x, c, rel_bias, norm_g, w_ada, b_ada, w_in, q_gain, k_gain, a_re, a_im, log_dt,
           b_re, b_im, c_re, c_im, d_skip, w_glu, b_glu, w_out):
    bsz, seq, d = x.shape
    depth = w_in.shape[0]
    width = d // 2
    n_heads = width // HEAD_DIM
    topk = min(TOPK_MAX, seq // 4)
    assert seq % min(ATT_QUERIES, seq) == 0 and ATT_QUERIES % ATT_TILE == 0
    assert seq % ATT_TILE == 0 and seq % SSM_CHUNK == 0
    assert ATT_TILE + 1 >= MAX_DISTANCE
    tz = _bias_prep(rel_bias, n_heads)
    for l in range(depth):
        mod = _adaln_mod(c, w_ada[l], b_ada[l])
        u, zs, k, kidx, qt, vt, qit, azt, wt = _in_proj(x, mod, norm_g[l], w_in[l],
                                                        q_gain[l], k_gain[l])
        tabs = _ssm_prep(a_re[l], a_im[l], log_dt[l], b_re[l], b_im[l])
        y_attn = _dsa_mixer(k, vt, kidx, qt, qit, azt, wt, tz, topk)
        x = _s5_mixer_out(u, zs, tabs, c_re[l], c_im[l], d_skip[l], w_glu[l], b_glu[l],
                          x, y_attn, w_out[l], mod)
    return x
```

```python
import functools
import math

import jax
import jax.numpy as jnp
import numpy as np
from jax import lax
from jax.experimental import pallas as pl
from jax.experimental.pallas import tpu as pltpu

F32 = jnp.float32
BF16 = jnp.bfloat16

SSM_GROUP = 16
SSM_STATE = 64
HEAD_DIM = 64
VAUG = 16
IDX_HEADS = 8
IDX_DIM = 64
TOPK_MAX = 256
NUM_BUCKETS = 32
MAX_DISTANCE = 128
EPS = 1e-6

LANES = 128
V7X_VMEM_LIMIT_BYTES = 56 * 1024 * 1024

SSM_CHUNK = 128
SSM_TILE = 512
SSM_SUB = 256
SSM_BLOCK_CH = 128
PROJ_TILE = 512
ATT_TILE = 256
ATT_QUERIES = 512
BISECT_MIN_ITERS = 14
BISECT_MAX_ITERS = 26
NEG = -(2.0 ** 100)
LOG2E = math.log2(math.e)

NT_DIMS = (((1,), (1,)), ((), ()))


def _cparams(sem):
    return pltpu.CompilerParams(dimension_semantics=sem,
                                vmem_limit_bytes=V7X_VMEM_LIMIT_BYTES)


def _sigmoid(x):
    return 1.0 / (1.0 + jnp.exp(-x))


def _silu(x):
    return x * _sigmoid(x)


def _ssm_prep_kernel(are_r, aim_r, ldt_r, are_c, aim_c, ldt_c, bre, bim,
                     ep_re, ep_im, em_re, em_im, laml_re, laml_im, bb_re, bb_im):
    dt = jnp.exp(ldt_r[...])
    ar = are_r[...] * dt
    ai = aim_r[...] * dt
    n = ar.shape[-1]
    tau = lax.broadcasted_iota(jnp.int32, (SSM_CHUNK, n), 0).astype(F32)
    mag_p = jnp.exp(tau * ar)
    mag_m = jnp.exp(-tau * ar)
    cs = jnp.cos(tau * ai)
    sn = jnp.sin(tau * ai)
    ep_re[...] = mag_p * cs
    ep_im[...] = mag_p * sn
    em_re[...] = mag_m * cs
    em_im[...] = -(mag_m * sn)
    mag_l = jnp.exp(float(SSM_CHUNK) * ar)
    laml_re[...] = mag_l * jnp.cos(float(SSM_CHUNK) * ai)
    laml_im[...] = mag_l * jnp.sin(float(SSM_CHUNK) * ai)
    a = are_c[...]
    b = aim_c[...]
    dtc = jnp.exp(ldt_c[...])
    mag = jnp.exp(a * dtc)
    x = mag * jnp.cos(b * dtc) - 1.0
    y = mag * jnp.sin(b * dtc)
    den = a * a + b * b
    cre = (x * a + y * b) / den
    cim = (y * a - x * b) / den
    bb_re[...] = cre * bre[...] - cim * bim[...]
    bb_im[...] = cre * bim[...] + cim * bre[...]


def _ssm_prep(a_re, a_im, log_dt, b_re, b_im):
    g, p = a_re.shape
    n = g * p
    ldt = jnp.broadcast_to(log_dt[:, None], (g, p))
    row = lambda t: t.reshape(1, n)
    col = lambda t: t.reshape(n, 1)
    tab = jax.ShapeDtypeStruct((SSM_CHUNK, n), F32)
    vec = jax.ShapeDtypeStruct((1, n), F32)
    bsh = jax.ShapeDtypeStruct((n, SSM_GROUP), F32)
    return pl.pallas_call(
        _ssm_prep_kernel,
        out_shape=(tab, tab, tab, tab, vec, vec, bsh, bsh),
        name="ssm_prep",
    )(row(a_re), row(a_im), row(ldt), col(a_re), col(a_im), col(ldt),
      b_re.reshape(n, SSM_GROUP), b_im.reshape(n, SSM_GROUP))


def _bias_prep_kernel(rb_ref, o_ref):
    t = o_ref.shape[-1]
    n_heads = o_ref.shape[1]
    i = lax.broadcasted_iota(jnp.int32, (t, t), 0)
    j = lax.broadcasted_iota(jnp.int32, (t, t), 1)
    max_exact = NUM_BUCKETS // 2

    def bucket_of(idx):
        dist = jnp.maximum(idx * t + j - i, 0)
        d = jnp.maximum(dist, 1).astype(F32)
        large = max_exact + (jnp.log(d / max_exact) / math.log(MAX_DISTANCE / max_exact)
                             * (NUM_BUCKETS - max_exact)).astype(jnp.int32)
        large = jnp.minimum(large, NUM_BUCKETS - 1)
        return jnp.where(dist < max_exact, dist, large)

    def bias(bucket, h):
        acc = jnp.zeros((t, t), F32)
        for b in range(NUM_BUCKETS):
            acc = jnp.where(bucket == b, rb_ref[b, h], acc)
        return acc

    buckets = [bucket_of(idx) for idx in range(3)]
    for h in range(n_heads):
        far = bias(buckets[2], h)
        for idx in range(3):
            o_ref[idx, h] = ((bias(buckets[idx], h) - far) * LOG2E).astype(o_ref.dtype)


def _bias_prep(rel_bias, n_heads):
    t = ATT_TILE
    return pl.pallas_call(
        _bias_prep_kernel,
        out_shape=jax.ShapeDtypeStruct((3, n_heads, t, t), BF16),
        in_specs=[pl.BlockSpec(memory_space=pltpu.SMEM)],
        name="bias_prep",
    )(rel_bias)


def _mod_kernel(c_ref, w_ref, b_ref, o_ref):
    c = c_ref[...]
    cond = _silu(c)
    o_ref[...] = jnp.dot(cond.astype(BF16), w_ref[...].astype(BF16),
                         preferred_element_type=F32) + b_ref[...]


def _adaln_mod(c, w_ada, b_ada):
    bsz, d = c.shape
    n = w_ada.shape[1]
    tn = 512
    return pl.pallas_call(
        _mod_kernel,
        out_shape=jax.ShapeDtypeStruct((bsz, n), F32),
        grid=(n // tn,),
        in_specs=[pl.BlockSpec((bsz, d), lambda i: (0, 0)),
                  pl.BlockSpec((d, tn), lambda i: (0, i)),
                  pl.BlockSpec((1, tn), lambda i: (0, i))],
        out_specs=pl.BlockSpec((bsz, tn), lambda i: (0, i)),
        compiler_params=_cparams(("arbitrary",)),
        name="adaln_mod",
    )(c, w_ada, b_ada.reshape(1, n))


def _in_proj_kernel(x_ref, mod_ref, g_ref, wnat_ref, wt_ref, ww_ref, mblk_ref,
                    kg_ref, qg_ref,
                    u_ref, zs_ref, k_ref, kidx_ref, qt_ref, vt_ref, qit_ref, azt_ref, wt_out_ref,
                    *, d_model, width, n_heads):
    x = x_ref[0]
    shift = mod_ref[0, :, 0:d_model]
    scale = mod_ref[0, :, d_model:2 * d_model]
    ms = jnp.mean(x * x, axis=-1, keepdims=True)
    xn = x * lax.rsqrt(ms + EPS) * g_ref[...]
    h = (xn * (1.0 + scale) + shift).astype(BF16)
    tm = h.shape[0]

    nat = jnp.dot(h, wnat_ref[...], preferred_element_type=F32)
    u_ref[0] = nat[:, 0:width].astype(BF16)
    zs_ref[0] = nat[:, width:2 * width].astype(BF16)
    k = nat[:, 2 * width:3 * width]
    kms = jnp.dot((k * k).astype(BF16), mblk_ref[...], preferred_element_type=F32)
    k_ref[0] = (k * lax.rsqrt(kms + EPS) * kg_ref[...]).astype(BF16)
    kidx_ref[0] = nat[:, 3 * width:3 * width + LANES].astype(BF16)

    tr = lax.dot_general(wt_ref[...], h, NT_DIMS, preferred_element_type=F32)
    q3 = tr[0:width].reshape(n_heads, HEAD_DIM, tm)
    qms = jnp.mean(q3 * q3, axis=1, keepdims=True)
    qn = q3 * lax.rsqrt(qms + EPS) * qg_ref[...][None] * (HEAD_DIM ** -0.5 * LOG2E)
    qt_ref[0] = qn.reshape(width, tm).astype(BF16)
    v3 = tr[width:2 * width].astype(BF16).reshape(n_heads, HEAD_DIM, tm)
    aug = lax.broadcasted_iota(jnp.int32, (n_heads, VAUG, tm), 1)
    aug = jnp.where(aug == 0, 1.0, 0.0).astype(BF16)
    vt_ref[0] = jnp.concatenate([v3, aug], axis=1).reshape(n_heads * (HEAD_DIM + VAUG), tm)
    qit_ref[0] = tr[2 * width:3 * width].astype(BF16)
    azt_ref[0] = tr[3 * width:4 * width].astype(BF16)
    wt_out_ref[0] = lax.dot_general(ww_ref[...], h, NT_DIMS,
                                    preferred_element_type=F32) * (IDX_HEADS ** -0.5)


def _in_proj(x, mod, norm_g, w_in, q_gain, k_gain):
    bsz, seq, d = x.shape
    width = d // 2
    n_heads = width // HEAD_DIM
    tm = min(PROJ_TILE, seq)
    o = np.cumsum([0, width, width, width, width, width, width,
                   IDX_HEADS * IDX_DIM, IDX_DIM, IDX_HEADS])
    col = lambda i: w_in[:, int(o[i]):int(o[i + 1])]
    w_kidx = jnp.pad(col(7), ((0, 0), (0, LANES - IDX_DIM)))
    w_nat = jnp.concatenate([col(0), col(1), col(3), w_kidx], axis=1).astype(BF16)
    w_t = jnp.concatenate([col(2), col(4), col(6), col(5)], axis=1).T.astype(BF16)
    w_w = col(8).T.astype(BF16)
    hid = np.arange(width) // HEAD_DIM
    mblk = jnp.asarray((hid[:, None] == hid[None, :]).astype(np.float32) / HEAD_DIM, BF16)
    kg = jnp.tile(k_gain, n_heads).reshape(1, width)
    qg = q_gain.reshape(HEAD_DIM, 1)

    nat_w = w_nat.shape[1]
    bs = jax.ShapeDtypeStruct
    const = lambda shape: pl.BlockSpec(shape, lambda b, s: (0,) * len(shape))
    tok = lambda w: pl.BlockSpec((1, tm, w), lambda b, s: (b, s, 0))
    trn = lambda r: pl.BlockSpec((1, r, tm), lambda b, s: (b, 0, s))
    kern = functools.partial(_in_proj_kernel, d_model=d, width=width, n_heads=n_heads)
    vrows = n_heads * (HEAD_DIM + VAUG)
    return pl.pallas_call(
        kern,
        out_shape=(bs((bsz, seq, width), BF16), bs((bsz, seq, width), BF16),
                   bs((bsz, seq, width), BF16), bs((bsz, seq, LANES), BF16),
                   bs((bsz, width, seq), BF16), bs((bsz, vrows, seq), BF16),
                   bs((bsz, width, seq), BF16), bs((bsz, width, seq), BF16),
                   bs((bsz, IDX_HEADS, seq), F32)),
        grid=(bsz, seq // tm),
        in_specs=[tok(d),
                  pl.BlockSpec((1, 1, 3 * d), lambda b, s: (b, 0, 0)),
                  const((1, d)), const((d, nat_w)), const((4 * width, d)),
                  const((IDX_HEADS, d)), const((width, width)),
                  const((1, width)), const((HEAD_DIM, 1))],
        out_specs=(tok(width), tok(width), tok(width), tok(LANES),
                   trn(width), trn(vrows), trn(width), trn(width), trn(IDX_HEADS)),
        compiler_params=_cparams(("parallel", "arbitrary")),
        name="in_proj",
    )(x, mod.reshape(bsz, 1, 3 * d), norm_g.reshape(1, d), w_nat, w_t, w_w, mblk, kg, qg)


def _s5_kernel(u_ref, zs_ref, bblk_ref, cre_ref, cim_ref, ep_re, ep_im, em_re, em_im,
               laml_re, laml_im, tri_ref, dsk_ref, wglu_ref, bglu_ref,
               xres_ref, ya_ref, w1_ref, w2_ref, mod_ref,
               o_ref, car_re, car_im, w_ref, x_ref, *, n_blocks, d_model):
    tm = u_ref.shape[1]
    sub = tri_ref.shape[0]
    nch = sub // SSM_CHUNK
    nst = cre_ref.shape[1]

    @pl.when(pl.program_id(1) == 0)
    def _():
        car_re[...] = jnp.zeros_like(car_re)
        car_im[...] = jnp.zeros_like(car_im)

    tri = tri_ref[...]
    halves = [slice(r * sub, (r + 1) * sub) for r in range(tm // sub)]
    for rs in halves:
        for j in range(n_blocks):
            st = slice(j * nst, (j + 1) * nst)
            ch = slice(j * SSM_BLOCK_CH, (j + 1) * SSM_BLOCK_CH)
            p = jnp.dot(u_ref[0, rs, ch], bblk_ref[j], preferred_element_type=F32)
            p_re = p[:, :nst].astype(BF16).reshape(nch, SSM_CHUNK, nst)
            p_im = p[:, nst:].astype(BF16).reshape(nch, SSM_CHUNK, nst)
            emr = em_re[:, st][None]
            emi = em_im[:, st][None]
            w_ref[j, rs, 0:nst] = (p_re * emr - p_im * emi).reshape(sub, nst)
            w_ref[j, rs, nst:2 * nst] = (p_re * emi + p_im * emr).reshape(sub, nst)
    for rs in halves:
        for j in range(n_blocks):
            st = slice(j * nst, (j + 1) * nst)
            cum = jnp.dot(tri, w_ref[j, rs, :], preferred_element_type=F32)
            cum_re = cum[:, :nst].reshape(nch, SSM_CHUNK, nst)
            cum_im = cum[:, nst:].reshape(nch, SSM_CHUNK, nst)
            lr = laml_re[:, st]
            li = laml_im[:, st]
            cr = car_re[:, st]
            ci = car_im[:, st]
            crs, cis = [], []
            for c in range(nch):
                crs.append(cr)
                cis.append(ci)
                tr_ = cum_re[c, SSM_CHUNK - 1:SSM_CHUNK, :] + cr
                ti_ = cum_im[c, SSM_CHUNK - 1:SSM_CHUNK, :] + ci
                cr = lr * tr_ - li * ti_
                ci = lr * ti_ + li * tr_
            car_re[:, st] = cr
            car_im[:, st] = ci
            a_re = (cum_re + jnp.stack(crs, axis=0)).astype(BF16)
            a_im = (cum_im + jnp.stack(cis, axis=0)).astype(BF16)
            epr = ep_re[:, st][None]
            epi = ep_im[:, st][None]
            x_ref[j, rs, 0:nst] = (a_re * epr - a_im * epi).reshape(sub, nst)
            x_ref[j, rs, nst:2 * nst] = (a_re * epi + a_im * epr).reshape(sub, nst)
    ys = []
    for j in range(n_blocks):
        ys.append(jnp.dot(x_ref[j, :, 0:nst], cre_ref[j], preferred_element_type=F32)
                  - jnp.dot(x_ref[j, :, nst:2 * nst], cim_ref[j], preferred_element_type=F32))
    y = jnp.concatenate(ys, axis=1) + dsk_ref[...] * u_ref[0].astype(F32)
    z = 0.5 * y * (1.0 + jnp.tanh(math.sqrt(2.0 / math.pi) * (y + 0.044715 * (y * y * y))))
    gl = jnp.dot(z.astype(BF16), wglu_ref[...], preferred_element_type=F32) + bglu_ref[...]
    zz = z * _sigmoid(gl)
    y_ssm = (zz * _silu(zs_ref[0].astype(F32))).astype(BF16)
    gate = mod_ref[0, :, 2 * d_model:3 * d_model]
    yo = (jnp.dot(y_ssm, w1_ref[...], preferred_element_type=F32)
          + jnp.dot(ya_ref[0], w2_ref[...], preferred_element_type=F32))
    o_ref[0] = xres_ref[0] + gate * yo


def _s5_mixer_out(u, zs, tabs, c_re, c_im, d_skip, w_glu, b_glu, x, y_attn, w_out, mod):
    bsz, seq, width = u.shape
    d = x.shape[-1]
    ep_re, ep_im, em_re, em_im, laml_re, laml_im, bb_re, bb_im = tabs
    tm = min(SSM_TILE, seq)
    gpb = SSM_BLOCK_CH // SSM_GROUP
    n_blocks = width // SSM_BLOCK_CH
    nst = gpb * SSM_STATE
    eye = jnp.eye(gpb, dtype=F32)

    def b_block(t):
        t4 = t.reshape(n_blocks, gpb, SSM_STATE, SSM_GROUP)
        return jnp.einsum('jgpc,gh->jgchp', t4, eye).reshape(n_blocks, SSM_BLOCK_CH, nst)

    def c_block(t):
        t4 = t.reshape(n_blocks, gpb, SSM_GROUP, SSM_STATE)
        return jnp.einsum('jgcp,gh->jhpgc', t4, eye).reshape(n_blocks, nst, SSM_BLOCK_CH)

    bblk = jnp.concatenate([b_block(bb_re), b_block(bb_im)], axis=2).astype(BF16)
    cre = c_block(c_re).astype(BF16)
    cim = c_block(c_im).astype(BF16)
    sub = min(SSM_SUB, tm)
    t_idx = np.arange(sub)
    tri = jnp.asarray(((t_idx[:, None] >= t_idx[None, :])
                       & (t_idx[:, None] // SSM_CHUNK == t_idx[None, :] // SSM_CHUNK)
                       ).astype(np.float32), BF16)
    n = ep_re.shape[1]
    const = lambda shape: pl.BlockSpec(shape, lambda b, s: (0,) * len(shape))
    tok = pl.BlockSpec((1, tm, width), lambda b, s: (b, s, 0))
    tokd = pl.BlockSpec((1, tm, d), lambda b, s: (b, s, 0))
    kern = functools.partial(_s5_kernel, n_blocks=n_blocks, d_model=d)
    return pl.pallas_call(
        kern,
        out_shape=jax.ShapeDtypeStruct((bsz, seq, d), x.dtype),
        grid=(bsz, seq // tm),
        in_specs=[tok, tok,
                  const((n_blocks, SSM_BLOCK_CH, 2 * nst)),
                  const((n_blocks, nst, SSM_BLOCK_CH)), const((n_blocks, nst, SSM_BLOCK_CH)),
                  const((SSM_CHUNK, n)), const((SSM_CHUNK, n)),
                  const((SSM_CHUNK, n)), const((SSM_CHUNK, n)),
                  const((1, n)), const((1, n)), const((sub, sub)),
                  const((1, width)), const((width, width)), const((1, width)),
                  tokd, tok, const((width, d)), const((width, d)),
                  pl.BlockSpec((1, 1, 3 * d), lambda b, s: (b, 0, 0))],
        out_specs=tokd,
        scratch_shapes=[pltpu.VMEM((1, n), F32), pltpu.VMEM((1, n), F32),
                        pltpu.VMEM((n_blocks, tm, 2 * nst), BF16),
                        pltpu.VMEM((n_blocks, tm, 2 * nst), BF16)],
        compiler_params=_cparams(("parallel", "arbitrary")),
        name="s5_mixer_out",
    )(u, zs, bblk, cre, cim, ep_re.astype(BF16), ep_im.astype(BF16),
      em_re.astype(BF16), em_im.astype(BF16), laml_re, laml_im, tri,
      d_skip.reshape(1, width), w_glu.astype(BF16), b_glu.reshape(1, width),
      x, y_attn, w_out[:width].astype(BF16), w_out[width:].astype(BF16),
      mod.reshape(bsz, 1, 3 * d))


def _dsa_kernel(k_ref, vt_ref, kidx_ref, qt_ref, qit_ref, azt_ref, wt_ref, tz_ref,
                o_ref, sc_ref, qip_ref, qp_ref, m_ref, acc_ref, mask_ref, s_ref, s2_ref,
                mt_ref, mt2_ref,
                p_ref, p2_ref, c_ref, c2_ref, sel_ref,
                *, topk, n_heads):
    t = ATT_TILE
    tq = qt_ref.shape[2]
    kpq = tq // t
    qi = pl.program_id(1)
    n_far = qi * kpq
    nk = n_far + kpq
    seq = k_ref.shape[1]
    row = lax.broadcasted_iota(jnp.int32, (t, tq), 0)
    colq = lax.broadcasted_iota(jnp.int32, (t, tq), 1)
    lane = lax.broadcasted_iota(jnp.int32, (1, tq), 1)
    pos = qi * tq + lane
    kq = jnp.minimum(pos + 1, topk).astype(F32)

    def tile_rows(kt):
        return pl.ds(pl.multiple_of(kt * t, t), t)

    zpad = jnp.zeros((HEAD_DIM, tq), BF16)
    for h in range(IDX_HEADS):
        qip_ref[h] = jnp.concatenate([qit_ref[0, h * IDX_DIM:(h + 1) * IDX_DIM, :], zpad], axis=0)
    for h in range(n_heads):
        qh = qt_ref[0, h * HEAD_DIM:(h + 1) * HEAD_DIM, :]
        qp_ref[h] = jnp.concatenate([qh, zpad] if h % 2 == 0 else [zpad, qh], axis=0)

    def score_tile(kt, lane0=0):
        kx = kidx_ref[0, tile_rows(kt), :]
        s = jnp.zeros((t, tq - lane0), F32)
        for h in range(IDX_HEADS):
            rel = jnp.dot(kx, qip_ref[h, :, lane0:], preferred_element_type=F32)
            s = s + wt_ref[0, h:h + 1, lane0:] * jnp.maximum(rel, 0.0)
        return s

    def score_far(kt, carry):
        smin, smax = carry
        s = score_tile(kt)
        sc_ref[tile_rows(kt), :] = s
        return (jnp.minimum(smin, jnp.min(s, axis=0, keepdims=True)),
                jnp.maximum(smax, jnp.max(s, axis=0, keepdims=True)))

    smin, smax = lax.fori_loop(
        0, n_far, score_far,
        (jnp.full((1, tq), jnp.inf, F32), jnp.full((1, tq), -jnp.inf, F32)))
    for d in range(kpq):
        l0 = d * t
        s = score_tile(n_far + d, l0)
        causal = (lax.broadcasted_iota(jnp.int32, (t, tq - l0), 0)
                  <= lax.broadcasted_iota(jnp.int32, (t, tq - l0), 1))
        rows_d = tile_rows(n_far + d)
        if l0:
            sc_ref[rows_d, :l0] = jnp.full((t, l0), -jnp.inf, F32)
        sc_ref[rows_d, l0:] = jnp.where(causal, s, -jnp.inf)
        lo_d = jnp.min(jnp.where(causal, s, jnp.inf), axis=0, keepdims=True)
        hi_d = jnp.max(jnp.where(causal, s, -jnp.inf), axis=0, keepdims=True)
        if l0:
            lo_d = jnp.concatenate([jnp.full((1, l0), jnp.inf, F32), lo_d], axis=1)
            hi_d = jnp.concatenate([jnp.full((1, l0), -jnp.inf, F32), hi_d], axis=1)
        smin = jnp.minimum(smin, lo_d)
        smax = jnp.maximum(smax, hi_d)

    def count_where(pred_fn):
        def tile_count(kt, lane0):
            m = pred_fn(sc_ref[tile_rows(kt), lane0:], lane0)
            return jnp.sum(jnp.where(m, 1.0, 0.0).reshape(4, t // 32, 8, tq - lane0), axis=1)

        def body(j, acc):
            for d in range(kpq):
                acc = acc + tile_count(j * kpq + d, 0)
            return acc
        acc = lax.fori_loop(0, qi, body, jnp.zeros((4, 8, tq), F32))
        total = jnp.sum((acc + tile_count(n_far, 0)).reshape(32, tq), axis=0, keepdims=True)
        for d in range(1, kpq):
            part = jnp.sum(tile_count(n_far + d, d * t).reshape(32, tq - d * t),
                           axis=0, keepdims=True)
            total = total + jnp.concatenate([jnp.zeros((1, d * t), F32), part], axis=1)
        return total

    def lanes_from(thr, lane0):
        if isinstance(thr, float) or lane0 == 0:
            return thr
        sel_ref[4:5, :] = thr
        return sel_ref[4:5, lane0:]

    def count_ge(thr):
        return count_where(lambda s_, l0: s_ >= lanes_from(thr, l0))

    def count_gt(thr):
        return count_where(lambda s_, l0: s_ > lanes_from(thr, l0))

    def min_where(pred_fn):
        def body(kt, acc):
            s = sc_ref[tile_rows(kt), :]
            return jnp.minimum(acc, jnp.min(jnp.where(pred_fn(s), s, jnp.inf), axis=0, keepdims=True))
        return lax.fori_loop(0, nk, body, jnp.full((1, tq), jnp.inf, F32))

    def bisect(st):
        lo, hi, c_lo = st
        mid = lo + 0.5 * (hi - lo)
        cnt = count_ge(mid)
        up = cnt >= kq
        return jnp.where(up, mid, lo), jnp.where(up, hi, mid), jnp.where(up, cnt, c_lo)

    c_ge0 = count_ge(0.0)
    c_gt0 = count_gt(0.0)
    above = c_gt0 >= kq
    below = c_ge0 < kq
    at_zero = jnp.logical_not(above | below)
    zero_tie = at_zero & (c_ge0 > kq)
    lo_neg = above & (smin <= 0.0)
    lo0 = jnp.where(at_zero | lo_neg, 0.0, smin)
    c_lo0 = jnp.where(at_zero | lo_neg, c_ge0, (pos + 1).astype(F32))
    hi0 = jnp.where(at_zero, 0.0, jnp.where(below, jnp.minimum(smax, 0.0), smax))
    open_f = jnp.where(zero_tie, 0.0, 1.0)

    st = lax.fori_loop(0, BISECT_MIN_ITERS, lambda _, s_: bisect(s_), (lo0, hi0, c_lo0))

    def more_cond(c):
        return (c[3] > 0.0) & (c[4] < BISECT_MAX_ITERS)

    def more_body(c):
        st_ = (c[0], c[1], c[2])
        flag = jnp.max((c[2] - kq) * open_f)
        return bisect(st_) + (flag, c[4] + 1)

    lo, _, c_lo, _, _ = lax.while_loop(
        more_cond, more_body, st + (jnp.float32(1.0), jnp.int32(BISECT_MIN_ITERS)))

    sel_ref[0:1, :] = lo
    sel_ref[1:2, :] = jnp.zeros((1, tq), F32)
    sel_ref[2:3, :] = kq - c_gt0
    sel_ref[3:4, :] = jnp.where(zero_tie, 1.0, 0.0)

    @pl.when(jnp.max((c_lo - kq) * open_f) > 0.0)
    def _():
        def peel_cond(c):
            return c[6] > 0.0

        def peel(c):
            lo_, c_lo_, taup_, ngt_, gtie_ = c[0], c[1], c[2], c[3], c[4]
            taup_n = min_where(lambda s_: s_ >= lo_)
            ngt_n = count_gt(taup_n)
            active = (c_lo_ > kq) & (open_f > 0.0) & (gtie_ == 0.0)
            is_tie = active & (ngt_n < kq)
            need_peel = active & (ngt_n >= kq)
            nxt = min_where(lambda s_: s_ > taup_n)
            return (jnp.where(need_peel, nxt, lo_), jnp.where(need_peel, ngt_n, c_lo_),
                    jnp.where(is_tie, taup_n, taup_), jnp.where(is_tie, ngt_n, ngt_),
                    jnp.where(is_tie, 1.0, gtie_), c[5],
                    jnp.max(jnp.where(need_peel, 1.0, 0.0)))

        zero = jnp.zeros((1, tq), F32)
        lo2, _, taup2, ngt2, gtie2, _, _ = lax.while_loop(
            peel_cond, peel, (lo, c_lo, zero, zero, zero, zero, jnp.float32(1.0)))
        gt = gtie2 > 0.0
        sel_ref[0:1, :] = lo2
        sel_ref[1:2, :] = jnp.where(gt, taup2, sel_ref[1:2, :])
        sel_ref[2:3, :] = jnp.where(gt, kq - ngt2, sel_ref[2:3, :])
        sel_ref[3:4, :] = jnp.where(gt, 1.0, sel_ref[3:4, :])

    lo_f = sel_ref[0:1, :]
    tie_f = sel_ref[3:4, :]
    any_tie = jnp.max(tie_f)

    @pl.when(any_tie <= 0.0)
    def _():
        def body(kt, c):
            s_ = sc_ref[tile_rows(kt), :]
            mask_ref[tile_rows(kt), :] = jnp.where(s_ >= lo_f, 0.0, NEG).astype(BF16)
            return c
        lax.fori_loop(0, nk, body, 0)

    @pl.when(any_tie > 0.0)
    def _():
        taup = sel_ref[1:2, :]
        need = jnp.where(tie_f > 0.0, sel_ref[2:3, :], float(2 * seq))
        tr_ = lax.broadcasted_iota(jnp.int32, (t, t), 0)
        tc_ = lax.broadcasted_iota(jnp.int32, (t, t), 1)
        tri = jnp.where(tc_ <= tr_, 1.0, 0.0).astype(BF16)

        def body(kt, seen):
            s_ = sc_ref[tile_rows(kt), :]
            z = s_ == taup
            pref = jnp.dot(tri, jnp.where(z, 1.0, 0.0).astype(BF16), preferred_element_type=F32)
            keep = (s_ >= lo_f) & jnp.logical_not(z & (seen + pref > need))
            mask_ref[tile_rows(kt), :] = jnp.where(keep, 0.0, NEG).astype(BF16)
            return seen + pref[t - 1:t, :]
        lax.fori_loop(0, nk, body, jnp.zeros((1, tq), F32))

    va = HEAD_DIM + VAUG

    def logits_head(h, kt, s_buf, mt_buf, l0=0):
        rows = tile_rows(kt)
        pair = slice((h // 2) * 2 * HEAD_DIM, (h // 2 + 1) * 2 * HEAD_DIM)
        s = jnp.dot(k_ref[0, rows, pair], qp_ref[h, :, l0:], preferred_element_type=F32)
        s = s.astype(BF16) + mask_ref[rows, l0:]
        parts = []
        for g in range(l0 // t, kpq):
            near = jnp.clip(n_far + g - kt, 0, 2)
            parts.append(s[:, g * t - l0:(g + 1) * t - l0] + tz_ref[near, h])
        s = jnp.concatenate(parts, axis=1) if len(parts) > 1 else parts[0]
        s_buf[h, :, l0:] = s
        mt_buf[h:h + 1, l0:] = jnp.max(s, axis=0, keepdims=True).astype(F32)

    def softmax_head(h, s_buf, mt_buf, p_buf, corr_buf, l0=0):
        m_old = m_ref[h:h + 1, l0:]
        m_new = jnp.maximum(m_old, mt_buf[h:h + 1, l0:])
        corr_buf[h:h + 1, l0:] = jnp.exp2(m_old - m_new)
        p_buf[h, :, l0:] = jnp.exp2(s_buf[h, :, l0:] - m_new.astype(BF16))
        m_ref[h:h + 1, l0:] = m_new

    def pv_head(h, kt, p_buf, corr_buf, l0=0):
        rows = tile_rows(kt)
        hs = slice(h * va, (h + 1) * va)
        acc_ref[hs, l0:] = acc_ref[hs, l0:] * corr_buf[h:h + 1, l0:] + jnp.dot(
            vt_ref[0, hs, rows], p_buf[h, :, l0:], preferred_element_type=F32)

    def trip(kt_pv, p_old, c_old, kt_next, s_next, mt_next, s_cur, mt_cur, p_cur, c_cur):
        for h in range(n_heads):
            pv_head(h, kt_pv, p_old, c_old)
            logits_head(h, kt_next, s_next, mt_next)
            softmax_head(h, s_cur, mt_cur, p_cur, c_cur)

    m_ref[...] = jnp.full_like(m_ref, NEG)
    acc_ref[...] = jnp.zeros_like(acc_ref)
    p2_ref[...] = jnp.zeros_like(p2_ref)
    c2_ref[...] = jnp.ones_like(c2_ref)
    for h in range(n_heads):
        logits_head(h, 0, s_ref, mt_ref)

    def kv_body(j, c):
        i = 2 * j
        trip(jnp.maximum(i - 1, 0), p2_ref, c2_ref, i + 1, s2_ref, mt2_ref,
             s_ref, mt_ref, p_ref, c_ref)
        trip(i, p_ref, c_ref, i + 2, s_ref, mt_ref, s2_ref, mt2_ref, p2_ref, c2_ref)
        return c

    lax.fori_loop(0, qi, kv_body, 0)

    for h in range(n_heads):
        pv_head(h, jnp.maximum(n_far - 1, 0), p2_ref, c2_ref)
        logits_head(h, n_far + 1, s2_ref, mt2_ref, t)
        softmax_head(h, s_ref, mt_ref, p_ref, c_ref)
    for h in range(n_heads):
        pv_head(h, n_far, p_ref, c_ref)
        softmax_head(h, s2_ref, mt2_ref, p2_ref, c2_ref, t)
    for h in range(n_heads):
        pv_head(h, n_far + 1, p2_ref, c2_ref, t)

    acc = acc_ref[...].reshape(n_heads, va, tq)
    out = acc[:, :HEAD_DIM, :] * (1.0 / acc[:, HEAD_DIM:HEAD_DIM + 1, :])
    out = out.reshape(n_heads * HEAD_DIM, tq) * _silu(azt_ref[0].astype(F32))
    o_ref[0] = out.T.astype(BF16)


def _dsa_mixer(k, vt, kidx, qt, qit, azt, wt, tz, topk):
    bsz, seq, width = k.shape
    n_heads = width // HEAD_DIM
    t = ATT_TILE
    tq = min(ATT_QUERIES, seq)
    assert tq == 2 * t
    per_b = lambda shape: pl.BlockSpec((1,) + shape, lambda b, q: (b, 0, 0))
    qtile = lambda r: pl.BlockSpec((1, r, tq), lambda b, q: (b, 0, q))
    kern = functools.partial(_dsa_kernel, topk=topk, n_heads=n_heads)
    vrows = vt.shape[1]
    return pl.pallas_call(
        kern,
        out_shape=jax.ShapeDtypeStruct((bsz, seq, width), BF16),
        grid=(bsz, seq // tq),
        in_specs=[per_b((seq, width)), per_b((vrows, seq)), per_b((seq, LANES)),
                  qtile(width), qtile(width), qtile(width), qtile(IDX_HEADS),
                  pl.BlockSpec((3, n_heads, t, t), lambda b, q: (0, 0, 0, 0))],
        out_specs=pl.BlockSpec((1, tq, width), lambda b, q: (b, q, 0)),
        scratch_shapes=[pltpu.VMEM((seq, tq), F32),
                        pltpu.VMEM((IDX_HEADS, 2 * IDX_DIM, tq), BF16),
                        pltpu.VMEM((n_heads, 2 * HEAD_DIM, tq), BF16),
                        pltpu.VMEM((n_heads, tq), F32),
                        pltpu.VMEM((vrows, tq), F32),
                        pltpu.VMEM((seq, tq), BF16),
                        pltpu.VMEM((n_heads, t, tq), BF16), pltpu.VMEM((n_heads, t, tq), BF16),
                        pltpu.VMEM((n_heads, tq), F32), pltpu.VMEM((n_heads, tq), F32),
                        pltpu.VMEM((n_heads, t, tq), BF16), pltpu.VMEM((n_heads, t, tq), BF16),
                        pltpu.VMEM((n_heads, tq), F32), pltpu.VMEM((n_heads, tq), F32),
                        pltpu.VMEM((8, tq), F32)],
        compiler_params=_cparams(("parallel", "arbitrary")),
        name="dsa_mixer",
    )(k, vt, kidx, qt, qit, azt, wt, tz)


def kernel(x, c, rel_bias, norm_g, w_ada, b_ada, w_in, q_gain, k_gain, a_re, a_im, log_dt,
           b_re, b_im, c_re, c_im, d_skip, w_glu, b_glu, w_out):
    bsz, seq, d = x.shape
    depth = w_in.shape[0]
    width = d // 2
    n_heads = width // HEAD_DIM
    topk = min(TOPK_MAX, seq // 4)
    assert seq % min(ATT_QUERIES, seq) == 0 and ATT_QUERIES % ATT_TILE == 0
    assert seq % ATT_TILE == 0 and seq % SSM_CHUNK == 0
    assert ATT_TILE + 1 >= MAX_DISTANCE
    tz = _bias_prep(rel_bias, n_heads)
    for l in range(depth):
        mod = _adaln_mod(c, w_ada[l], b_ada[l])
        u, zs, k, kidx, qt, vt, qit, azt, wt = _in_proj(x, mod, norm_g[l], w_in[l],
                                                        q_gain[l], k_gain[l])
        tabs = _ssm_prep(a_re[l], a_im[l], log_dt[l], b_re[l], b_im[l])
        y_attn = _dsa_mixer(k, vt, kidx, qt, qit, azt, wt, tz, topk)
        x = _s5_mixer_out(u, zs, tabs, c_re[l], c_im[l], d_skip[l], w_glu[l], b_glu[l],
                          x, y_attn, w_out[l], mod)
    return x
```
